```python
import math
import jax, jax.numpy as jnp
from jax import lax
import numpy as np

D_MODEL = 1024
BATCH = 16
SEQ = 256
DEPTH = 2
DEC_BATCH = 8
DEC_SEQ = 4096
PAST_LEN = 256

GRID_W = 64
EPS = 1e-6
Q_BLOCK = 128
DA_H = 4
DA_HD = 64
DA_W = DA_H * 2 * DA_HD
ROPE_BASE = 10000.0
SSM_H = 8
SSM_P = 64
SSM_G = 2
SSM_HG = SSM_H // SSM_G
SSM_N = 128
SSM_DI = SSM_H * SSM_P
SSM_CHUNK = 128
CONV_W = 5
CONV_CH = SSM_DI + 2 * SSM_G * SSM_N
NA_H = 8
NA_HD = 64
NA_W = NA_H * NA_HD
WIN_R = 8
WIN_C = 16
QB_C = 16
IN_SIZES = (DA_W, DA_W, DA_W, SSM_DI, CONV_CH, 2 * SSM_H, NA_W, NA_W, NA_W)
IN_W = 3 * DA_W + SSM_DI + CONV_CH + 2 * SSM_H + 3 * NA_W
N_EXPERTS = 32
TOP_K = 4
EXPERT_FF = 128
SHARED_FF = 256
ROUTE_SCALE = 2.5

kernel_name = "hybrid_diffusion_trunk_step"


def rmsnorm(x, g):
    xf = x.astype(jnp.float32)
    y = xf * lax.rsqrt(jnp.mean(xf * xf, axis=-1, keepdims=True) + EPS)
    return (y * g.astype(jnp.float32)).astype(x.dtype)


def split_in(u):
    offs, acc = [], 0
    for s in IN_SIZES[:-1]:
        acc += s
        offs.append(acc)
    return jnp.split(u, offs, axis=-1)


def centred_dwconv(x, w, b):
    k = w.shape[0]
    y = lax.conv_general_dilated(x, w[:, None, :].astype(x.dtype), window_strides=(1,),
                                 padding=[(k // 2, k // 2)], dimension_numbers=("NWC", "WIO", "NWC"),
                                 feature_group_count=x.shape[-1])
    return y + b.astype(x.dtype)


def axial_rope_tables(n_tok):
    pos = jnp.arange(n_tok)
    rows = (pos // GRID_W).astype(jnp.float32)
    cols = (pos % GRID_W).astype(jnp.float32)
    n_freq = DA_HD // 4
    inv = ROPE_BASE ** (-jnp.arange(n_freq, dtype=jnp.float32) / n_freq)
    ar = rows[:, None] * inv
    ac = cols[:, None] * inv
    return (jnp.cos(ar), jnp.sin(ar), jnp.cos(ac), jnp.sin(ac))


def rot_half(x, cos, sin):
    x1, x2 = jnp.split(x, 2, axis=-1)
    return jnp.concatenate([x1 * cos - x2 * sin, x2 * cos + x1 * sin], axis=-1)


def apply_axial_rope(x, tabs):
    cr, sr, cc, sc = [t[:, None, :] for t in tabs]
    xr, xc = jnp.split(x.astype(jnp.float32), 2, axis=-1)
    return jnp.concatenate([rot_half(xr, cr, sr), rot_half(xc, cc, sc)], axis=-1).astype(x.dtype)


def diff_lambda(lam_p, layer):
    lam_init = 0.8 - 0.6 * math.exp(-0.3 * layer)
    lf = lam_p.astype(jnp.float32)
    lam = jnp.exp(jnp.sum(lf[0] * lf[1])) - jnp.exp(jnp.sum(lf[2] * lf[3])) + lam_init
    return lam, lam_init


def diff_attention(q, k, v, lam):
    bn, h, lq, _, d = q.shape
    nb = lq // Q_BLOCK
    qb = q.reshape(bn, h, nb, Q_BLOCK, 2, d).transpose(2, 0, 1, 3, 4, 5)
    scale = d ** -0.5

    def block(qi):
        s = jnp.einsum("bhqtd,bhktd->tbhqk", qi, k).astype(jnp.float32) * scale
        pr = jax.nn.softmax(s, axis=-1)
        a = pr[0] - lam * pr[1]
        return jnp.einsum("bhqk,bhkv->bhqv", a.astype(v.dtype), v)

    o = lax.map(block, qb)
    return o.transpose(1, 2, 0, 3, 4).reshape(bn, h, lq, 2 * d)


def block_attention(q, k, v):
    bn, h, lq, d = q.shape
    nb = lq // Q_BLOCK
    qb = q.reshape(bn, h, nb, Q_BLOCK, d).transpose(2, 0, 1, 3, 4)
    scale = d ** -0.5

    def block(qi):
        s = jnp.einsum("bhqd,bhkd->bhqk", qi, k).astype(jnp.float32) * scale
        return jnp.einsum("bhqk,bhkd->bhqd", jax.nn.softmax(s, axis=-1).astype(v.dtype), v)

    o = lax.map(block, qb)
    return o.transpose(1, 2, 0, 3, 4).reshape(bn, h, lq, d)


def neighbourhood_attention(q, k, v, kc, vc, rpb):
    bn, h, n, d = q.shape
    rows = n // GRID_W
    wr, wc = min(WIN_R, rows), WIN_C
    qr = math.gcd(rows, wr)
    kr = min(wr + qr - 1, rows)
    qc = QB_C
    kcw = min(wc + qc - 1, GRID_W)
    nrb, ncb = rows // qr, GRID_W // qc
    q_rows = np.arange(nrb)[:, None] * qr + np.arange(qr)
    key_rows = np.clip(np.arange(nrb) * qr - wr // 2, 0, rows - kr)[:, None] + np.arange(kr)
    q_cols = np.arange(ncb)[:, None] * qc + np.arange(qc)
    key_cols = np.clip(np.arange(ncb) * qc - wc // 2, 0, GRID_W - kcw)[:, None] + np.arange(kcw)
    s_r = np.clip(q_rows - wr // 2, 0, rows - wr)[:, :, None]
    s_c = np.clip(q_cols - wc // 2, 0, GRID_W - wc)[:, :, None]
    ok_r = (key_rows[:, None, :] >= s_r) & (key_rows[:, None, :] < s_r + wr)
    ok_c = (key_cols[:, None, :] >= s_c) & (key_cols[:, None, :] < s_c + wc)
    off_r = np.clip(key_rows[:, None, :] - q_rows[:, :, None] + WIN_R - 1, 0, 2 * WIN_R - 2)
    off_c = np.clip(key_cols[:, None, :] - q_cols[:, :, None] + WIN_C - 1, 0, 2 * WIN_C - 2)
    ri = lambda a: a[:, None, :, None, :, None]
    ci = lambda a: a[None, :, None, :, None, :]
    nq_blk, nk_blk = qr * qc, kr * kcw
    bias = rpb.astype(jnp.float32)[:, ri(off_r), ci(off_c)]
    bias = jnp.where(jnp.asarray(ri(ok_r) & ci(ok_c)), bias, -jnp.inf)
    bias = bias.reshape(h, nrb, ncb, nq_blk, nk_blk).transpose(1, 0, 2, 3, 4)
    qb = q.reshape(bn, h, nrb, qr, ncb, qc, d).transpose(2, 0, 1, 4, 3, 5, 6).reshape(nrb, bn, h, ncb, nq_blk, d)
    k5 = k.reshape(bn, h, rows, GRID_W, d)
    v5 = v.reshape(bn, h, rows, GRID_W, d)
    kcols = jnp.asarray(key_cols, jnp.int32)
    scale = d ** -0.5

    def gather(t5, kri):
        g = jnp.take(jnp.take(t5, kri, axis=2), kcols, axis=3)
        return g.transpose(0, 1, 3, 2, 4, 5).reshape(bn, h, ncb, nk_blk, d)

    def block(args):
        qi, kri, bi = args
        kg, vg = gather(k5, kri), gather(v5, kri)
        s_w = jnp.einsum("bhjqd,bhjkd->bhjqk", qi, kg).astype(jnp.float32) * scale + bi[None]
        s_x = jnp.einsum("bhjqd,bhcd->bhjqc", qi, kc).astype(jnp.float32) * scale
        pr = jax.nn.softmax(jnp.concatenate([s_w, s_x], axis=-1), axis=-1).astype(v.dtype)
        return (jnp.einsum("bhjqk,bhjkd->bhjqd", pr[..., :nk_blk], vg)
                + jnp.einsum("bhjqc,bhcd->bhjqd", pr[..., nk_blk:], vc))

    o = lax.map(block, (qb, jnp.asarray(key_rows, jnp.int32), bias))
    o = o.reshape(nrb, bn, h, ncb, qr, qc, d).transpose(1, 2, 0, 4, 3, 5, 6)
    return o.reshape(bn, h, n, d)


def segsum(x):
    t = x.shape[-1]
    xx = jnp.broadcast_to(x[..., :, None], x.shape + (t,))
    ss = jnp.cumsum(jnp.where(jnp.tril(jnp.ones((t, t), bool), -1), xx, 0.0), axis=-2)
    return jnp.where(jnp.tril(jnp.ones((t, t), bool)), ss, -jnp.inf)


def ssd_chunked(x, a, b, c, h0):
    bn, l = x.shape[0], x.shape[1]
    t = SSM_CHUNK
    nc = l // t
    x = x.reshape(bn, nc, t, SSM_G, SSM_HG, SSM_P)
    a = a.reshape(bn, nc, t, SSM_G, SSM_HG).transpose(0, 3, 4, 1, 2)
    b = b.astype(jnp.float32).reshape(bn, nc, t, SSM_G, SSM_N)
    c = c.astype(jnp.float32).reshape(bn, nc, t, SSM_G, SSM_N)
    a_cum = jnp.cumsum(a, axis=-1)
    lmat = jnp.exp(segsum(a))
    y_diag = jnp.einsum("bclgn,bcsgn,bghcls,bcsghp->bclghp", c, b, lmat, x)
    decay_states = jnp.exp(a_cum[..., -1:] - a_cum)
    states = jnp.einsum("bclgn,bghcl,bclghp->bcghpn", b, decay_states, x)
    states = jnp.concatenate([h0.astype(jnp.float32)[:, None], states], axis=1)
    chunk_tot = jnp.pad(a_cum[..., -1], ((0, 0), (0, 0), (0, 0), (1, 0)))
    decay_chunk = jnp.exp(segsum(chunk_tot))
    new_states = jnp.einsum("bghzc,bcghpn->bzghpn", decay_chunk, states)
    states, final = new_states[:, :-1], new_states[:, -1]
    y_off = jnp.einsum("bclgn,bcghpn,bghcl->bclghp", c, states, jnp.exp(a_cum))
    return (y_diag + y_off).reshape(bn, l, SSM_G, SSM_HG, SSM_P), final


def ssm_branch(z, xbc, dt, p, h0_f, h0_b):
    bn, l, _ = z.shape
    xbc = jax.nn.silu(centred_dwconv(xbc, p["ssm_conv_w"], p["ssm_conv_b"]))
    xs = xbc[..., :SSM_DI].reshape(bn, l, SSM_G, SSM_HG, SSM_P).astype(jnp.float32)
    bs = xbc[..., SSM_DI:SSM_DI + SSM_G * SSM_N].reshape(bn, l, SSM_G, SSM_N)
    cs = xbc[..., SSM_DI + SSM_G * SSM_N:].reshape(bn, l, SSM_G, SSM_N)
    dt = jax.nn.softplus(dt.astype(jnp.float32).reshape(bn, l, 2, SSM_H) + p["ssm_dt_bias"].astype(jnp.float32))
    a = -jnp.exp(p["ssm_a_log"].astype(jnp.float32))

    def direction(di, xs_, bs_, cs_, dt_, h0):
        dtd = dt_[:, :, di].reshape(bn, l, SSM_G, SSM_HG)
        y, hf = ssd_chunked(xs_ * dtd[..., None], dtd * a[di].reshape(SSM_G, SSM_HG), bs_, cs_,
                            h0.reshape(bn, SSM_G, SSM_HG, SSM_P, SSM_N))
        return y, hf.reshape(bn, SSM_H, SSM_P, SSM_N)

    y_f, h_f = direction(0, xs, bs, cs, dt, h0_f)
    y_b, h_b = direction(1, xs[:, ::-1], bs[:, ::-1], cs[:, ::-1], dt[:, ::-1], h0_b)
    y = y_f + y_b[:, ::-1] + xs * p["ssm_d"].astype(jnp.float32).reshape(SSM_G, SSM_HG, 1)
    y = y.reshape(bn, l, SSM_DI).astype(z.dtype) * jax.nn.silu(z)
    y = rmsnorm(y.reshape(bn, l, SSM_G, SSM_DI // SSM_G), p["ssm_norm"].reshape(SSM_G, -1)).reshape(bn, l, SSM_DI)
    return y, h_f.astype(z.dtype), h_b.astype(z.dtype)


def moe_ffn(h, p):
    bn, l, d = h.shape
    t = h.reshape(-1, d)
    s = jax.nn.sigmoid((t @ p["w_router"]).astype(jnp.float32))
    _, idx = lax.top_k(s + p["b_router"].astype(jnp.float32), TOP_K)
    sel = jnp.take_along_axis(s, idx, axis=-1)
    w = sel / jnp.sum(sel, axis=-1, keepdims=True) * ROUTE_SCALE
    gates = jnp.einsum("nk,nke->ne", w, jax.nn.one_hot(idx, N_EXPERTS, dtype=jnp.float32)).astype(h.dtype)
    hg = jnp.einsum("nd,edf->nef", t, p["w_e_gate"])
    hu = jnp.einsum("nd,edf->nef", t, p["w_e_up"])
    routed = jnp.einsum("nef,efd->nd", jax.nn.silu(hg) * hu * gates[:, :, None], p["w_e_down"])
    shared = (jax.nn.silu(t @ p["w_s_gate"]) * (t @ p["w_s_up"])) @ p["w_s_down"]
    return (routed + shared).reshape(bn, l, d)


def trunk_layer(x, cvec, p, layer, ctx=None):
    bn, n, _ = x.shape
    mods = jax.nn.silu(cvec) @ p["w_ada"] + p["b_ada"]
    sh1, sc1, g1, sh2, sc2, g2 = jnp.split(mods[:, None, :], 6, axis=-1)
    h = rmsnorm(x, p["g_pre1"]) * (1.0 + sc1) + sh1
    dq, dk, dv, sz, sxbc, sdt, nq, nk, nv = split_in(h @ p["w_in"])
    dq = dq.reshape(bn, n, DA_H, 2, DA_HD).transpose(0, 2, 1, 3, 4)
    dk = dk.reshape(bn, n, DA_H, 2, DA_HD).transpose(0, 2, 1, 3, 4)
    dv = dv.reshape(bn, n, DA_H, 2 * DA_HD).transpose(0, 2, 1, 3)
    nq, nk, nv = [t.reshape(bn, n, NA_H, NA_HD).transpose(0, 2, 1, 3) for t in (nq, nk, nv)]
    lam, lam_init = diff_lambda(p["da_lambda"], layer)
    if ctx is None:
        h0_f = jnp.zeros((bn, SSM_H, SSM_P, SSM_N), x.dtype)
        h0_b = h0_f
        o_da = diff_attention(dq, dk, dv, lam)
        o_na = block_attention(nq, nk, nv)
    else:
        ck, cv, cnk, cnv, h0_f, h0_b = ctx
        tabs = axial_rope_tables(n)
        o_da = diff_attention(apply_axial_rope(dq, tabs),
                              jnp.concatenate([ck, apply_axial_rope(dk, tabs)], axis=2),
                              jnp.concatenate([cv, dv], axis=2), lam)
        o_na = neighbourhood_attention(nq, nk, nv, cnk, cnv, p["na_rpb"])
    o_da = (rmsnorm(o_da, p["da_subln"]) * (1.0 - lam_init)).transpose(0, 2, 1, 3).reshape(bn, n, DA_W)
    o_na = o_na.transpose(0, 2, 1, 3).reshape(bn, n, NA_W)
    o_ssm, hf, hb = ssm_branch(sz, sxbc, sdt, p, h0_f, h0_b)
    ga, gs, gn = jnp.split(jax.nn.sigmoid(h @ p["w_gate"]), 3, axis=-1)
    mix = (ga * (o_da @ p["w_br_da"]) + gs * (o_ssm @ p["w_br_ssm"]) + gn * (o_na @ p["w_br_na"])) @ p["w_out"]
    x = x + g1 * rmsnorm(mix, p["g_post1"])
    h2 = rmsnorm(x, p["g_pre2"]) * (1.0 + sc2) + sh2
    x = x + g2 * rmsnorm(moe_ffn(h2, p), p["g_post2"])
    state = (dk, dv, nk, nv, hf, hb) if ctx is None else None
    return x, state


def setup_inputs(seed: int = 0) -> dict:
    key = jax.random.key(seed)
    ks = iter(jax.random.split(key, 48))
    nrm = lambda shape, s: jax.random.normal(next(ks), shape, jnp.float32) * s
    L, D = DEPTH, D_MODEL
    out = {}
    out["x_prompt"] = nrm((BATCH, SEQ, D), 1.0)
    out["x_sample"] = nrm((DEC_BATCH, DEC_SEQ, D), 1.0)
    out["c"] = nrm((DEC_BATCH, D), 1.0)
    out["cache_da_k"] = nrm((DEC_BATCH, L, DA_H, PAST_LEN, 2, DA_HD), 1.0)
    out["cache_da_v"] = nrm((DEC_BATCH, L, DA_H, PAST_LEN, 2 * DA_HD), 1.0)
    out["cache_na_k"] = nrm((DEC_BATCH, L, NA_H, PAST_LEN, NA_HD), 1.0)
    out["cache_na_v"] = nrm((DEC_BATCH, L, NA_H, PAST_LEN, NA_HD), 1.0)
    out["state_ssm_f"] = nrm((DEC_BATCH, L, SSM_H, SSM_P, SSM_N), 0.5)
    out["state_ssm_b"] = nrm((DEC_BATCH, L, SSM_H, SSM_P, SSM_N), 0.5)
    out["c_ctx"] = nrm((D,), 1.0)
    out["w_ada"] = nrm((L, D, 6 * D), 0.5 * D ** -0.5)
    out["b_ada"] = nrm((L, 6 * D), 0.02)
    out["g_pre1"] = 1.0 + nrm((L, D), 0.05)
    out["g_post1"] = 1.0 + nrm((L, D), 0.05)
    out["g_pre2"] = 1.0 + nrm((L, D), 0.05)
    out["g_post2"] = 1.0 + nrm((L, D), 0.05)
    out["w_in"] = nrm((L, D, IN_W), D ** -0.5)
    out["w_gate"] = nrm((L, D, 3 * D), D ** -0.5)
    out["da_lambda"] = nrm((L, 4, DA_HD), 0.1)
    out["da_subln"] = 1.0 + nrm((L, 2 * DA_HD), 0.05)
    out["ssm_conv_w"] = nrm((L, CONV_W, CONV_CH), CONV_W ** -0.5)
    out["ssm_conv_b"] = nrm((L, CONV_CH), 0.02)
    dt = jnp.exp(jax.random.uniform(next(ks), (L, 2, SSM_H), jnp.float32, math.log(1e-3), math.log(1e-1)))
    out["ssm_dt_bias"] = dt + jnp.log(-jnp.expm1(-dt))
    out["ssm_a_log"] = jnp.log(jax.random.uniform(next(ks), (L, 2, SSM_H), jnp.float32, 1.0, 16.0))
    out["ssm_d"] = 1.0 + nrm((L, SSM_H), 0.1)
    out["ssm_norm"] = 1.0 + nrm((L, SSM_DI), 0.05)
    out["na_rpb"] = nrm((L, NA_H, 2 * WIN_R - 1, 2 * WIN_C - 1), 0.05)
    out["w_br_da"] = nrm((L, DA_W, D), DA_W ** -0.5)
    out["w_br_ssm"] = nrm((L, SSM_DI, D), SSM_DI ** -0.5)
    out["w_br_na"] = nrm((L, NA_W, D), NA_W ** -0.5)
    out["w_out"] = nrm((L, D, D), D ** -0.5)
    out["w_router"] = nrm((L, D, N_EXPERTS), D ** -0.5)
    out["b_router"] = nrm((L, N_EXPERTS), 0.01)
    out["w_e_gate"] = nrm((L, N_EXPERTS, D, EXPERT_FF), D ** -0.5)
    out["w_e_up"] = nrm((L, N_EXPERTS, D, EXPERT_FF), D ** -0.5)
    out["w_e_down"] = nrm((L, N_EXPERTS, EXPERT_FF, D), EXPERT_FF ** -0.5)
    out["w_s_gate"] = nrm((L, D, SHARED_FF), D ** -0.5)
    out["w_s_up"] = nrm((L, D, SHARED_FF), D ** -0.5)
    out["w_s_down"] = nrm((L, SHARED_FF, D), SHARED_FF ** -0.5)
    return out


def reference(x_prompt, x_sample, c, cache_da_k, cache_da_v, cache_na_k, cache_na_v, state_ssm_f, state_ssm_b,
              c_ctx, w_ada, b_ada, g_pre1, g_post1, g_pre2, g_post2, w_in, w_gate, da_lambda, da_subln,
              ssm_conv_w, ssm_conv_b, ssm_dt_bias, ssm_a_log, ssm_d, ssm_norm, na_rpb, w_br_da, w_br_ssm,
              w_br_na, w_out, w_router, b_router, w_e_gate, w_e_up, w_e_down, w_s_gate, w_s_up, w_s_down):
    stacked = {
        "w_ada": w_ada, "b_ada": b_ada, "g_pre1": g_pre1, "g_post1": g_post1, "g_pre2": g_pre2,
        "g_post2": g_post2, "w_in": w_in, "w_gate": w_gate, "da_lambda": da_lambda, "da_subln": da_subln,
        "ssm_conv_w": ssm_conv_w, "ssm_conv_b": ssm_conv_b, "ssm_dt_bias": ssm_dt_bias, "ssm_a_log": ssm_a_log,
        "ssm_d": ssm_d, "ssm_norm": ssm_norm, "na_rpb": na_rpb, "w_br_da": w_br_da, "w_br_ssm": w_br_ssm,
        "w_br_na": w_br_na, "w_out": w_out, "w_router": w_router, "b_router": b_router, "w_e_gate": w_e_gate,
        "w_e_up": w_e_up, "w_e_down": w_e_down, "w_s_gate": w_s_gate, "w_s_up": w_s_up, "w_s_down": w_s_down,
    }
    xp, xs = x_prompt, x_sample
    cvec_ctx = c_ctx[None]
    new = ([], [], [], [], [], [])
    for layer in range(DEPTH):
        p = {name: arr[layer] for name, arr in stacked.items()}
        xp, st = trunk_layer(xp, cvec_ctx, p, layer)
        for lst, arr in zip(new, st):
            lst.append(arr)
        ctx = (cache_da_k[:, layer], cache_da_v[:, layer], cache_na_k[:, layer], cache_na_v[:, layer],
               state_ssm_f[:, layer], state_ssm_b[:, layer])
        xs, _ = trunk_layer(xs, c, p, layer, ctx)
    new_da_k = jnp.stack(new[0], axis=1)
    new_da_v = jnp.stack(new[1], axis=1)
    new_na_k = jnp.stack(new[2], axis=1)
    new_na_v = jnp.stack(new[3], axis=1)
    new_ssm_f = jnp.stack(new[4], axis=1)
    new_ssm_b = jnp.stack(new[5], axis=1)
    return (xp, xs, new_da_k, new_da_v, new_na_k, new_na_v, new_ssm_f, new_ssm_b)
```

```python
import functools
import math

import numpy as np
import jax
import jax.numpy as jnp
from jax import lax
from jax.experimental import pallas as pl
from jax.experimental.pallas import tpu as pltpu

F32 = jnp.float32
BF16 = jnp.bfloat16

D_MODEL = 1024
GRID_W = 64
EPS = 1e-6
DA_H, DA_HD = 4, 64
DA_W = DA_H * 2 * DA_HD
ROPE_BASE = 10000.0
SSM_H, SSM_P, SSM_G, SSM_N = 8, 64, 2, 128
SSM_DI = SSM_H * SSM_P
SSM_CHUNK = 128
CONV_W = 5
CONV_CH = SSM_DI + 2 * SSM_G * SSM_N
NA_H, NA_HD = 8, 64
NA_W = NA_H * NA_HD
WIN_R, WIN_C = 8, 16
N_EXPERTS, TOP_K, EXPERT_FF, SHARED_FF = 32, 4, 128, 256
ROUTE_SCALE = 2.5

LANES = 128
SUBLANES = 8
VMEM_LIMIT_BYTES = 56 * 2**20

PRE_TM = 256
POST_TM = 512
MOE_TM = 512
MOE_EG = 8
DA_TQ = 256
DA_TK = 512
NA_QB = WIN_R * GRID_W
NA_KROWS = 16
NA_KW = NA_KROWS * GRID_W
DT_PAD = LANES

OFF_DQ, OFF_DK, OFF_DV, OFF_SZ, OFF_XBC, OFF_NQ, OFF_NK, OFF_NV, OFF_DT = (
    0, 512, 1024, 1536, 2048, 3072, 3584, 4096, 4608)
IN_WP = OFF_DT + DT_PAD

_NT = (((1,), (1,)), ((), ()))


def _cparams(sem):
    return pltpu.CompilerParams(dimension_semantics=sem, vmem_limit_bytes=VMEM_LIMIT_BYTES)


def _bdot(a, b):
    return jnp.dot(a.astype(BF16), b.astype(BF16), preferred_element_type=F32)


def _bdot_nt(a, b):
    return lax.dot_general(a.astype(BF16), b.astype(BF16), _NT, preferred_element_type=F32)


def _fdot(a, b):
    return jnp.dot(a, b, preferred_element_type=F32, precision=lax.Precision.HIGHEST)


def _sigmoid(x):
    return 1.0 / (1.0 + jnp.exp(-x))


def _silu(x):
    return x * _sigmoid(x)


def _softplus(x):
    return jnp.maximum(x, 0.0) + jnp.log(1.0 + jnp.exp(-jnp.abs(x)))


def _rms(x, g):
    return x * lax.rsqrt(jnp.mean(x * x, axis=-1, keepdims=True) + EPS) * g


def _ada_kernel(c_ref, w_ref, b_ref, o_ref):
    o_ref[...] = _bdot(_silu(c_ref[...]), w_ref[...]) + b_ref[...]


def _ada(cvec, w_ada, b_ada):
    nl, d, d6 = w_ada.shape
    r = cvec.shape[0]
    tn = 1024
    return pl.pallas_call(
        _ada_kernel,
        grid=(nl, d6 // tn),
        in_specs=[pl.BlockSpec((r, d), lambda l, j: (0, 0)),
                  pl.BlockSpec((None, d, tn), lambda l, j: (l, 0, j)),
                  pl.BlockSpec((None, 1, tn), lambda l, j: (l, 0, j))],
        out_specs=pl.BlockSpec((None, r, tn), lambda l, j: (l, 0, j)),
        out_shape=jax.ShapeDtypeStruct((nl, r, d6), F32),
        compiler_params=_cparams(("arbitrary", "arbitrary")),
        name="ada",
    )(cvec, w_ada, b_ada.reshape(nl, 1, d6))


def _pre_kernel(*refs, rope):
    if rope:
        x_ref, mod_ref, g_ref, win_ref, wg_ref, cos_ref, sa_ref, sb_ref = refs[:8]
        outs = refs[8:]
    else:
        x_ref, mod_ref, g_ref, win_ref, wg_ref = refs[:5]
        outs = refs[5:]
    dq_ref, dk_ref, dv_ref, sz_ref, xbc_ref, nq_ref, nk_ref, nv_ref, dt_ref, gate_ref = outs
    x = x_ref[...]
    h = _rms(x, g_ref[...]) * (1.0 + mod_ref[1:2, :]) + mod_ref[0:1, :]
    hb = h.astype(BF16)

    def proj(off, width):
        return jnp.dot(hb, win_ref[:, off:off + width], preferred_element_type=F32)

    for off, ref in ((OFF_DQ, dq_ref), (OFF_DK, dk_ref)):
        if rope:
            cos, sa, sb = cos_ref[...], sa_ref[...], sb_ref[...]
            for hd in range(DA_H):
                u = proj(off + hd * LANES, LANES)
                u = u * cos + pltpu.roll(u, LANES - 16, 1) * sa + pltpu.roll(u, 16, 1) * sb
                ref[:, hd * LANES:(hd + 1) * LANES] = u.astype(ref.dtype)
        else:
            ref[...] = proj(off, DA_W).astype(ref.dtype)
    dv_ref[...] = proj(OFF_DV, DA_W).astype(dv_ref.dtype)
    sz_ref[...] = proj(OFF_SZ, SSM_DI)
    xbc_ref[...] = proj(OFF_XBC, CONV_CH)
    nq_ref[...] = proj(OFF_NQ, NA_W).astype(nq_ref.dtype)
    nk_ref[...] = proj(OFF_NK, NA_W).astype(nk_ref.dtype)
    nv_ref[...] = proj(OFF_NV, NA_W).astype(nv_ref.dtype)
    dt_ref[...] = proj(OFF_DT, DT_PAD)
    gate_ref[...] = _sigmoid(jnp.dot(hb, wg_ref[...], preferred_element_type=F32))


def _pre(x3, mods, g_pre1, w_in_p, w_gate, rope_tabs, qkv_dtype):
    b, n, d = x3.shape
    tm = PRE_TM
    tps = n // tm
    t = b * n
    x2 = x3.reshape(t, d)
    per_batch_mods = mods.shape[0] > 1
    mod_idx = (lambda i: (i // tps, 0, 0)) if per_batch_mods else (lambda i: (0, 0, 0))
    const2 = lambda i: (0, 0)
    row = lambda i: (i, 0)
    in_specs = [pl.BlockSpec((tm, d), row),
                pl.BlockSpec((None, 6, d), mod_idx),
                pl.BlockSpec((1, d), const2),
                pl.BlockSpec((d, IN_WP), const2, pipeline_mode=pl.Buffered(1)),
                pl.BlockSpec((d, 3 * d), const2, pipeline_mode=pl.Buffered(1))]
    args = [x2, mods, g_pre1.reshape(1, d), w_in_p, w_gate]
    rope = rope_tabs is not None
    if rope:
        in_specs += [pl.BlockSpec((tm, LANES), lambda i: (i % tps, 0))] * 3
        args += list(rope_tabs)
    widths = (DA_W, DA_W, DA_W, SSM_DI, CONV_CH, NA_W, NA_W, NA_W, DT_PAD, 3 * d)
    dtypes = (qkv_dtype, qkv_dtype, qkv_dtype, F32, F32, qkv_dtype, qkv_dtype, qkv_dtype, F32, F32)
    outs = pl.pallas_call(
        functools.partial(_pre_kernel, rope=rope),
        grid=(t // tm,),
        in_specs=in_specs,
        out_specs=[pl.BlockSpec((tm, w), row) for w in widths],
        out_shape=[jax.ShapeDtypeStruct((t, w), dt) for w, dt in zip(widths, dtypes)],
        compiler_params=_cparams(("arbitrary",)),
        name="pre_rope" if rope else "pre",
    )(*args)
    return [o.reshape(b, n, o.shape[-1]) for o in outs]


def _da_kernel(*refs, lam_init, has_ctx, tk):
    if has_ctx:
        lam_ref, sub_ref, q_ref, k_ref, v_ref, ck_ref, cv_ref, o_ref, s_ref, sc_ref = refs
    else:
        lam_ref, sub_ref, q_ref, k_ref, v_ref, o_ref, s_ref = refs
    tq = q_ref.shape[0]
    nc = k_ref.shape[0] // tk
    ng = tk // LANES
    lf = lam_ref[...]
    lam = (jnp.exp(jnp.sum(lf[0:1] * lf[1:2], axis=-1, keepdims=True))
           - jnp.exp(jnp.sum(lf[2:3] * lf[3:4], axis=-1, keepdims=True)) + lam_init)
    q = q_ref[...].astype(F32) * (DA_HD ** -0.5)
    lane = lax.broadcasted_iota(jnp.int32, (1, LANES), 1)
    inv_l = []
    for t in (0, 1):
        qt = jnp.where((lane >= DA_HD) == bool(t), q, 0.0).astype(BF16)

        def qk_body(kc, m_acc, t=t, qt=qt):
            kb = k_ref[pl.ds(pl.multiple_of(kc * tk, tk), tk), :]
            s = _bdot_nt(qt, kb)
            s_ref[t, kc] = s
            for j in range(ng):
                m_acc = jnp.maximum(m_acc, s[:, j * LANES:(j + 1) * LANES])
            return m_acc

        m_acc = lax.fori_loop(0, nc, qk_body, jnp.full((tq, LANES), -jnp.inf, F32))
        if has_ctx:
            sc = _bdot_nt(qt, ck_ref[...])
            sc_ref[t] = sc
            for j in range(sc.shape[1] // LANES):
                m_acc = jnp.maximum(m_acc, sc[:, j * LANES:(j + 1) * LANES])
        m = jnp.max(m_acc, axis=-1, keepdims=True)

        def exp_body(kc, l_acc, t=t, m=m):
            p = jnp.exp(s_ref[t, kc] - m)
            s_ref[t, kc] = p
            for j in range(ng):
                l_acc = l_acc + p[:, j * LANES:(j + 1) * LANES]
            return l_acc

        l_acc = lax.fori_loop(0, nc, exp_body, jnp.zeros((tq, LANES), F32))
        if has_ctx:
            pc = jnp.exp(sc_ref[t] - m)
            sc_ref[t] = pc
            for j in range(pc.shape[1] // LANES):
                l_acc = l_acc + pc[:, j * LANES:(j + 1) * LANES]
        inv_l.append(1.0 / jnp.sum(l_acc, axis=-1, keepdims=True))
    c0 = inv_l[0]
    c1 = inv_l[1] * lam

    def pv_body(kc, acc):
        a = s_ref[0, kc] * c0 - s_ref[1, kc] * c1
        vb = v_ref[pl.ds(pl.multiple_of(kc * tk, tk), tk), :]
        return acc + _bdot(a, vb)

    acc = lax.fori_loop(0, nc, pv_body, jnp.zeros((tq, LANES), F32))
    if has_ctx:
        acc = acc + _bdot(sc_ref[0] * c0 - sc_ref[1] * c1, cv_ref[...])
    o = _rms(acc, sub_ref[...]) * (1.0 - lam_init)
    o_ref[...] = o.astype(o_ref.dtype)


def _da(dq, dk, dv, lam_p, subln, layer, ctx_kv):
    b, n, _ = dq.shape
    lam_init = 0.8 - 0.6 * math.exp(-0.3 * layer)
    tq = min(DA_TQ, n)
    tk = min(DA_TK, n)
    has_ctx = ctx_kv is not None
    qmap = lambda bi, h, qi: (bi, qi, h)
    kvmap = lambda bi, h, qi: (bi, 0, h)
    in_specs = [pl.BlockSpec((4, DA_HD), lambda bi, h, qi: (0, 0)),
                pl.BlockSpec((1, 2 * DA_HD), lambda bi, h, qi: (0, 0)),
                pl.BlockSpec((None, tq, LANES), qmap),
                pl.BlockSpec((None, n, LANES), kvmap),
                pl.BlockSpec((None, n, LANES), kvmap)]
    args = [lam_p, subln.reshape(1, 2 * DA_HD), dq, dk, dv]
    scratch = [pltpu.VMEM((2, n // tk, tq, tk), F32)]
    if has_ctx:
        ck, cv = ctx_kv
        past = ck.shape[3]
        cmap = lambda bi, h, qi: (bi, layer, h, 0, 0)
        in_specs += [pl.BlockSpec((None, None, None, past, LANES), cmap)] * 2
        args += [ck, cv]
        scratch.append(pltpu.VMEM((2, tq, past), F32))
    return pl.pallas_call(
        functools.partial(_da_kernel, lam_init=lam_init, has_ctx=has_ctx, tk=tk),
        grid=(b, DA_H, n // tq),
        in_specs=in_specs,
        out_specs=pl.BlockSpec((None, tq, LANES), qmap),
        out_shape=jax.ShapeDtypeStruct((b, n, DA_W), F32),
        scratch_shapes=scratch,
        compiler_params=_cparams(("arbitrary", "arbitrary", "arbitrary")),
        name="da_lat" if has_ctx else "da_ctx",
    )(*args)


def _pair_attend(q, sources):
    lane = lax.broadcasted_iota(jnp.int32, (1, LANES), 1)
    outs = []
    for j in (0, 1):
        qj = jnp.where((lane >= NA_HD) == bool(j), q, 0.0).astype(BF16)
        ss = []
        for k, _, bias in sources:
            s = _bdot_nt(qj, k)
            if bias is not None:
                s = s + bias[j]
            ss.append(s)
        m = ss[0].max(axis=-1, keepdims=True)
        for s in ss[1:]:
            m = jnp.maximum(m, s.max(axis=-1, keepdims=True))
        l = 0.0
        o = 0.0
        for s, (_, v, _) in zip(ss, sources):
            p = jnp.exp(s - m)
            l = l + p.sum(axis=-1, keepdims=True)
            o = o + _bdot(p, v)
        outs.append(o * (1.0 / l))
    return jnp.where(lane >= NA_HD, outs[1], outs[0])


def _na_lat_kernel(q_ref, k_ref, v_ref, ck_ref, cv_ref, b_ref, o_ref, *, n_rows):
    rb = pl.program_id(2)
    row0 = jnp.clip(rb * WIN_R - WIN_R // 2, 0, n_rows - NA_KROWS)
    start = pl.multiple_of(row0 * GRID_W, GRID_W)
    kw = k_ref[pl.ds(start, NA_KW), :]
    vw = v_ref[pl.ds(start, NA_KW), :]
    q = q_ref[...].astype(F32) * (NA_HD ** -0.5)
    o_ref[...] = _pair_attend(q, [(kw, vw, b_ref), (ck_ref[...], cv_ref[...], None)])


def _na_ctx_kernel(q_ref, k_ref, v_ref, o_ref):
    q = q_ref[...].astype(F32) * (NA_HD ** -0.5)
    o_ref[...] = _pair_attend(q, [(k_ref[...], v_ref[...], None)])


def _na_bias_tables(rpb, n_rows):
    nrb = n_rows // WIN_R
    tabs = []
    for rb in (0, 1, nrb - 1):
        row0 = min(max(rb * WIN_R - WIN_R // 2, 0), n_rows - NA_KROWS)
        q_rows = rb * WIN_R + np.arange(WIN_R)
        k_rows = row0 + np.arange(NA_KROWS)
        s_r = np.clip(q_rows - WIN_R // 2, 0, n_rows - WIN_R)[:, None]
        ok_r = (k_rows[None, :] >= s_r) & (k_rows[None, :] < s_r + WIN_R)
        off_r = np.clip(k_rows[None, :] - q_rows[:, None] + WIN_R - 1, 0, 2 * WIN_R - 2)
        cols = np.arange(GRID_W)
        s_c = np.clip(cols - WIN_C // 2, 0, GRID_W - WIN_C)[:, None]
        ok_c = (cols[None, :] >= s_c) & (cols[None, :] < s_c + WIN_C)
        off_c = np.clip(cols[None, :] - cols[:, None] + WIN_C - 1, 0, 2 * WIN_C - 2)
        ri = lambda a: a[:, None, :, None]
        ci = lambda a: a[None, :, None, :]
        bias = rpb.astype(F32)[:, ri(off_r), ci(off_c)]
        bias = jnp.where(jnp.asarray(ri(ok_r) & ci(ok_c)), bias, -jnp.inf)
        tabs.append(bias.reshape(rpb.shape[0], NA_QB, NA_KW))
    return jnp.stack(tabs, axis=1)


def _na_lat(nq, nk, nv, ck, cv, rpb, layer):
    b, n, _ = nq.shape
    n_rows = n // GRID_W
    assert n_rows >= NA_KROWS and n_rows % WIN_R == 0
    nrb = n_rows // WIN_R
    past = ck.shape[3]
    bias = _na_bias_tables(rpb, n_rows)
    kvmap = lambda bi, hp, rb: (bi, 0, hp)
    cmap = lambda bi, hp, rb: (bi, layer, hp, 0, 0)
    variant = lambda bi, hp, rb: (hp, jnp.where(rb == 0, 0, jnp.where(rb == nrb - 1, 2, 1)), 0, 0)
    return pl.pallas_call(
        functools.partial(_na_lat_kernel, n_rows=n_rows),
        grid=(b, NA_H // 2, nrb),
        in_specs=[pl.BlockSpec((None, NA_QB, LANES), lambda bi, hp, rb: (bi, rb, hp)),
                  pl.BlockSpec((None, n, LANES), kvmap),
                  pl.BlockSpec((None, n, LANES), kvmap),
                  pl.BlockSpec((None, None, None, past, LANES), cmap),
                  pl.BlockSpec((None, None, None, past, LANES), cmap),
                  pl.BlockSpec((2, None, NA_QB, NA_KW), variant)],
        out_specs=pl.BlockSpec((None, NA_QB, LANES), lambda bi, hp, rb: (bi, rb, hp)),
        out_shape=jax.ShapeDtypeStruct((b, n, NA_W), F32),
        compiler_params=_cparams(("arbitrary", "arbitrary", "arbitrary")),
        name="na_lat",
    )(nq, nk, nv, ck, cv, bias)


def _na_ctx(nq, nk, nv):
    b, n, _ = nq.shape
    spec = pl.BlockSpec((None, n, LANES), lambda bi, hp: (bi, 0, hp))
    return pl.pallas_call(
        _na_ctx_kernel,
        grid=(b, NA_H // 2),
        in_specs=[spec, spec, spec],
        out_specs=spec,
        out_shape=jax.ShapeDtypeStruct((b, n, NA_W), F32),
        compiler_params=_cparams(("arbitrary", "arbitrary")),
        name="na_ctx",
    )(nq, nk, nv)


def _ssd_kernel(*refs, nc, has_h0):
    if has_h0:
        (xm_ref, xp_ref, xn_ref, z_ref, dt_ref, dtt_ref, cw_ref, cb_ref, dtb_ref, dtbt_ref, alogt_ref,
         alogx_ref, dexp_ref, nrm_ref, h0f_ref, h0b_ref, y_ref, hf_ref, hb_ref, xpad_ref, st_ref, yacc_ref) = refs
    else:
        (xm_ref, xp_ref, xn_ref, z_ref, dt_ref, dtt_ref, cw_ref, cb_ref, dtb_ref, dtbt_ref, alogt_ref,
         alogx_ref, dexp_ref, nrm_ref, y_ref, hf_ref, hb_ref, xpad_ref, st_ref, yacc_ref) = refs
    j = pl.program_id(1)
    t = SSM_CHUNK
    gw = SSM_DI // SSM_G
    p_shift = SSM_P.bit_length() - 1
    bwd = j >= nc
    c = jnp.where(bwd, 2 * nc - 1 - j, j)

    @pl.when(j == 0)
    def _():
        st_ref[...] = h0f_ref[...] if has_h0 else jnp.zeros_like(st_ref)

    @pl.when(j == nc)
    def _():
        st_ref[...] = h0b_ref[...] if has_h0 else jnp.zeros_like(st_ref)

    xpad_ref[0:SUBLANES, :] = jnp.where(c == 0, 0.0, xp_ref[...])
    xpad_ref[SUBLANES:SUBLANES + t, :] = xm_ref[...]
    xpad_ref[SUBLANES + t:, :] = jnp.where(c == nc - 1, 0.0, xn_ref[...])
    conv = cb_ref[...]
    for k in range(CONV_W):
        o = SUBLANES - CONV_W // 2 + k
        conv = conv + cw_ref[k:k + 1, :] * xpad_ref[o:o + t, :]
    act = _silu(conv)
    xs = act[:, :SSM_DI]
    bm = act[:, SSM_DI:SSM_DI + SSM_G * SSM_N]
    cm = act[:, SSM_DI + SSM_G * SSM_N:]

    li = lax.broadcasted_iota(jnp.int32, (t, t), 0)
    si = lax.broadcasted_iota(jnp.int32, (t, t), 1)
    keep = jnp.where(bwd, li - si, si - li) <= 0
    tri = keep.astype(F32)
    trit = (jnp.where(bwd, si - li, li - si) <= 0).astype(F32)
    dirv = bwd.astype(jnp.int32)
    r_i = lax.broadcasted_iota(jnp.int32, (LANES, SSM_DI), 0)
    c_i = lax.broadcasted_iota(jnp.int32, (LANES, SSM_DI), 1)
    expand = (r_i == dirv * SSM_H + (c_i >> p_shift)).astype(F32)
    dtv = _softplus(dt_ref[...] + dtb_ref[...])
    dtx = _fdot(dtv, expand)
    a_exp = -jnp.exp(jnp.where(bwd, alogx_ref[1:2, :], alogx_ref[0:1, :]))
    acum = _fdot(tri, dtx * a_exp)
    tot = jnp.where(bwd, acum[0:1, :], acum[t - 1:t, :])
    e_in = jnp.exp(acum)
    e_out = jnp.exp(tot - acum)
    xdt = xs * dtx
    xdt_b = xdt.astype(BF16)
    xout_b = (xdt * e_out).astype(BF16)

    at = -jnp.exp(alogt_ref[...]) * _softplus(dtt_ref[...] + dtbt_ref[...])
    acum_t = _fdot(at, trit)

    lane_g = lax.broadcasted_iota(jnp.int32, (1, gw), 1)
    y_parts = []
    for g in range(SSM_G):
        cg = cm[:, g * SSM_N:(g + 1) * SSM_N].astype(BF16)
        bg = bm[:, g * SSM_N:(g + 1) * SSM_N]
        gmat = _bdot_nt(cg, bg)
        st_g = st_ref[:, g * gw:(g + 1) * gw]
        y_g = _bdot(cg, st_g) * e_in[:, g * gw:(g + 1) * gw]
        for hh in range(SSM_H // SSM_G):
            h = g * (SSM_H // SSM_G) + hh
            col = acum[:, h * SSM_P:h * SSM_P + 1]
            row = jnp.where(bwd, acum_t[SSM_H + h:SSM_H + h + 1, :], acum_t[h:h + 1, :])
            lmat = jnp.exp(jnp.where(keep, col - row, -jnp.inf))
            yh = _bdot(gmat * lmat, xdt_b[:, g * gw:(g + 1) * gw])
            y_g = y_g + jnp.where((lane_g >> p_shift) == hh, yh, 0.0)
        y_parts.append(y_g)
        st_new = st_g * jnp.exp(tot[:, g * gw:(g + 1) * gw]) + _bdot(bg.T, xout_b[:, g * gw:(g + 1) * gw])
        st_ref[:, g * gw:(g + 1) * gw] = st_new
    y_dir = jnp.concatenate(y_parts, axis=-1)

    row0 = pl.multiple_of(c * t, t)

    @pl.when(jnp.logical_not(bwd))
    def _():
        yacc_ref[pl.ds(row0, t), :] = y_dir + xs * dexp_ref[...]

    @pl.when(j == nc - 1)
    def _():
        hf_ref[...] = st_ref[...]

    @pl.when(bwd)
    def _():
        y = yacc_ref[pl.ds(row0, t), :] + y_dir
        y = y * _silu(z_ref[...])
        parts = [_rms(y[:, g * gw:(g + 1) * gw], nrm_ref[:, g * gw:(g + 1) * gw]) for g in range(SSM_G)]
        y_ref[...] = jnp.concatenate(parts, axis=-1)

    @pl.when(j == 2 * nc - 1)
    def _():
        hb_ref[...] = st_ref[...]


def _ssd(sz, sxbc, sdt, p, h0):
    b, n, _ = sz.shape
    t = SSM_CHUNK
    nc = n // t
    nb8 = n // SUBLANES
    tb = t // SUBLANES
    has_h0 = h0 is not None
    sdt_t = jnp.swapaxes(sdt[:, :, :2 * SSM_H], 1, 2)
    cidx = lambda j: jnp.where(j >= nc, 2 * nc - 1 - j, j)
    zidx = lambda j: jnp.where(j >= nc, 2 * nc - 1 - j, nc - 1)
    const2 = lambda bi, j: (0, 0)
    cw = jnp.zeros((SUBLANES, CONV_CH), F32).at[:CONV_W].set(p["ssm_conv_w"])
    dtb = jnp.zeros((1, DT_PAD), F32).at[0, :2 * SSM_H].set(p["ssm_dt_bias"].reshape(-1))
    dtbt = jnp.broadcast_to(p["ssm_dt_bias"].reshape(2 * SSM_H, 1), (2 * SSM_H, t))
    alogt = jnp.broadcast_to(p["ssm_a_log"].reshape(2 * SSM_H, 1), (2 * SSM_H, t))
    alogx = jnp.repeat(p["ssm_a_log"], SSM_P, axis=-1)
    dexp = jnp.repeat(p["ssm_d"], SSM_P).reshape(1, SSM_DI)
    in_specs = [pl.BlockSpec((None, t, CONV_CH), lambda bi, j: (bi, cidx(j), 0)),
                pl.BlockSpec((None, SUBLANES, CONV_CH), lambda bi, j: (bi, jnp.maximum(cidx(j) * tb - 1, 0), 0)),
                pl.BlockSpec((None, SUBLANES, CONV_CH),
                             lambda bi, j: (bi, jnp.minimum((cidx(j) + 1) * tb, nb8 - 1), 0)),
                pl.BlockSpec((None, t, SSM_DI), lambda bi, j: (bi, zidx(j), 0)),
                pl.BlockSpec((None, t, DT_PAD), lambda bi, j: (bi, cidx(j), 0)),
                pl.BlockSpec((None, 2 * SSM_H, t), lambda bi, j: (bi, 0, cidx(j))),
                pl.BlockSpec((SUBLANES, CONV_CH), const2),
                pl.BlockSpec((1, CONV_CH), const2),
                pl.BlockSpec((1, DT_PAD), const2),
                pl.BlockSpec((2 * SSM_H, t), const2),
                pl.BlockSpec((2 * SSM_H, t), const2),
                pl.BlockSpec((2, SSM_DI), const2),
                pl.BlockSpec((1, SSM_DI), const2),
                pl.BlockSpec((1, SSM_DI), const2)]
    args = [sxbc, sxbc, sxbc, sz, sdt, sdt_t, cw, p["ssm_conv_b"].reshape(1, CONV_CH), dtb, dtbt,
            alogt, alogx, dexp, p["ssm_norm"].reshape(1, SSM_DI)]
    st_spec = pl.BlockSpec((None, SSM_N, SSM_DI), lambda bi, j: (bi, 0, 0))
    if has_h0:
        in_specs += [st_spec, st_spec]
        args += list(h0)
    y, hf, hb = pl.pallas_call(
        functools.partial(_ssd_kernel, nc=nc, has_h0=has_h0),
        grid=(b, 2 * nc),
        in_specs=in_specs,
        out_specs=[pl.BlockSpec((None, t, SSM_DI), lambda bi, j: (bi, zidx(j), 0)), st_spec, st_spec],
        out_shape=[jax.ShapeDtypeStruct((b, n, SSM_DI), F32),
                   jax.ShapeDtypeStruct((b, SSM_N, SSM_DI), F32),
                   jax.ShapeDtypeStruct((b, SSM_N, SSM_DI), F32)],
        scratch_shapes=[pltpu.VMEM((t + 2 * SUBLANES, CONV_CH), F32),
                        pltpu.VMEM((SSM_N, SSM_DI), F32),
                        pltpu.VMEM((n, SSM_DI), F32)],
        compiler_params=_cparams(("arbitrary", "arbitrary")),
        name="ssd_lat" if has_h0 else "ssd_ctx",
    )(*args)
    return y, hf, hb


def _post_kernel(x_ref, oda_ref, ossm_ref, ona_ref, gate_ref, mod_ref, wda_ref, wssm_ref, wna_ref, wout_ref,
                 gpost_ref, gpre_ref, x1_ref, h2_ref):
    d = x_ref.shape[1]
    mix = (gate_ref[:, 0:d] * _bdot(oda_ref[...], wda_ref[...])
           + gate_ref[:, d:2 * d] * _bdot(ossm_ref[...], wssm_ref[...])
           + gate_ref[:, 2 * d:3 * d] * _bdot(ona_ref[...], wna_ref[...]))
    mix = _bdot(mix, wout_ref[...])
    x1 = x_ref[...] + mod_ref[2:3, :] * _rms(mix, gpost_ref[...])
    x1_ref[...] = x1
    h2_ref[...] = _rms(x1, gpre_ref[...]) * (1.0 + mod_ref[4:5, :]) + mod_ref[3:4, :]


def _post(x3, o_da, o_ssm, o_na, gates, mods, p):
    b, n, d = x3.shape
    tm = min(POST_TM, n)
    tps = n // tm
    t = b * n
    per_batch_mods = mods.shape[0] > 1
    mod_idx = (lambda i: (i // tps, 0, 0)) if per_batch_mods else (lambda i: (0, 0, 0))
    row = lambda i: (i, 0)
    const2 = lambda i: (0, 0)
    flat = lambda a: a.reshape(t, a.shape[-1])
    wspec = lambda k: pl.BlockSpec((k, d), const2)
    x1, h2 = pl.pallas_call(
        _post_kernel,
        grid=(t // tm,),
        in_specs=[pl.BlockSpec((tm, d), row), pl.BlockSpec((tm, DA_W), row), pl.BlockSpec((tm, SSM_DI), row),
                  pl.BlockSpec((tm, NA_W), row), pl.BlockSpec((tm, 3 * d), row),
                  pl.BlockSpec((None, 6, d), mod_idx),
                  wspec(DA_W), wspec(SSM_DI), wspec(NA_W), wspec(d),
                  pl.BlockSpec((1, d), const2), pl.BlockSpec((1, d), const2)],
        out_specs=[pl.BlockSpec((tm, d), row), pl.BlockSpec((tm, d), row)],
        out_shape=[jax.ShapeDtypeStruct((t, d), F32), jax.ShapeDtypeStruct((t, d), F32)],
        compiler_params=_cparams(("arbitrary",)),
        name="post",
    )(flat(x3), flat(o_da), flat(o_ssm), flat(o_na), flat(gates), mods,
      p["w_br_da"], p["w_br_ssm"], p["w_br_na"], p["w_out"],
      p["g_post1"].reshape(1, d), p["g_pre2"].reshape(1, d))
    return x1, h2


def _moe_kernel(x1_ref, h2_ref, mod_ref, wr_ref, br_ref, wsg_ref, wsu_ref, wsd_ref, weg_ref, weu_ref, wed_ref,
                gpost_ref, o_ref, acc_ref, gate_ref):
    e = pl.program_id(1)
    tm = h2_ref.shape[0]
    h2 = h2_ref[...]
    hb = h2.astype(BF16)

    @pl.when(e == 0)
    def _():
        s = _sigmoid(_fdot(h2, wr_ref[...]))
        work = s + br_ref[...]
        col = lax.broadcasted_iota(jnp.int32, (tm, LANES), 1).astype(F32)
        picked = jnp.zeros((tm, LANES), F32)
        for _ in range(TOP_K):
            m = jnp.max(work, axis=-1, keepdims=True)
            first = jnp.min(jnp.where(work == m, col, float(LANES)), axis=-1, keepdims=True)
            hit = col == first
            picked = jnp.where(hit, 1.0, picked)
            work = jnp.where(hit, -jnp.inf, work)
        sel = picked * s
        g = sel / jnp.sum(sel, axis=-1, keepdims=True) * ROUTE_SCALE
        g_hi = g.astype(BF16).astype(F32)
        r1 = g - g_hi
        g_mid = r1.astype(BF16).astype(F32)
        g_lo = r1 - g_mid
        gate_ref[...] = g_hi + pltpu.roll(g_mid, N_EXPERTS, 1) + pltpu.roll(g_lo, 2 * N_EXPERTS, 1)
        acc_ref[...] = _bdot(_silu(_bdot(hb, wsg_ref[...])) * _bdot(hb, wsu_ref[...]), wsd_ref[...])

    width = MOE_EG * EXPERT_FF
    r_i = lax.broadcasted_iota(jnp.int32, (LANES, width), 0)
    c_i = lax.broadcasted_iota(jnp.int32, (LANES, width), 1)
    ff_shift = EXPERT_FF.bit_length() - 1
    onehot = jnp.where((r_i < 3 * N_EXPERTS) & ((r_i & (N_EXPERTS - 1)) == e * MOE_EG + (c_i >> ff_shift)),
                       1.0, 0.0).astype(BF16)
    g_exp = jnp.dot(gate_ref[...].astype(BF16), onehot, preferred_element_type=F32)
    act = _silu(_bdot(hb, weg_ref[...])) * _bdot(hb, weu_ref[...]) * g_exp
    acc_ref[...] += _bdot(act, wed_ref[...])

    @pl.when(e == pl.num_programs(1) - 1)
    def _():
        o_ref[...] = x1_ref[...] + mod_ref[5:6, :] * _rms(acc_ref[...], gpost_ref[...])


def _moe(x1, h2, mods, p, n):
    t, d = x1.shape
    tm = MOE_TM
    per_batch_mods = mods.shape[0] > 1
    if per_batch_mods:
        assert n % tm == 0
    tps = max(n // tm, 1)
    mod_idx = (lambda i, e: (i // tps, 0, 0)) if per_batch_mods else (lambda i, e: (0, 0, 0))
    row = lambda i, e: (i, 0)
    const2 = lambda i, e: (0, 0)
    width = MOE_EG * EXPERT_FF
    w_router = jnp.zeros((d, LANES), F32).at[:, :N_EXPERTS].set(p["w_router"])
    b_router = jnp.full((1, LANES), -jnp.inf, F32).at[0, :N_EXPERTS].set(p["b_router"])
    return pl.pallas_call(
        _moe_kernel,
        grid=(t // tm, N_EXPERTS // MOE_EG),
        in_specs=[pl.BlockSpec((tm, d), row), pl.BlockSpec((tm, d), row),
                  pl.BlockSpec((None, 6, d), mod_idx),
                  pl.BlockSpec((d, LANES), const2), pl.BlockSpec((1, LANES), const2),
                  pl.BlockSpec((d, SHARED_FF), const2), pl.BlockSpec((d, SHARED_FF), const2),
                  pl.BlockSpec((SHARED_FF, d), const2),
                  pl.BlockSpec((d, width), lambda i, e: (0, e)),
                  pl.BlockSpec((d, width), lambda i, e: (0, e)),
                  pl.BlockSpec((width, d), lambda i, e: (e, 0)),
                  pl.BlockSpec((1, d), const2)],
        out_specs=pl.BlockSpec((tm, d), row),
        out_shape=jax.ShapeDtypeStruct((t, d), F32),
        scratch_shapes=[pltpu.VMEM((tm, d), F32), pltpu.VMEM((tm, LANES), F32)],
        compiler_params=_cparams(("arbitrary", "arbitrary")),
        name="moe",
    )(x1, h2, mods, w_router, b_router,
      p["w_s_gate"], p["w_s_up"], p["w_s_down"], p["w_e_gate"], p["w_e_up"], p["w_e_down"],
      p["g_post2"].reshape(1, d))


def _rope_tables(n):
    pos = jnp.arange(n)
    rows = (pos // GRID_W).astype(F32)
    cols = (pos % GRID_W).astype(F32)
    n_freq = DA_HD // 4
    inv = ROPE_BASE ** (-jnp.arange(n_freq, dtype=F32) / n_freq)
    ar, ac = rows[:, None] * inv, cols[:, None] * inv
    zero = jnp.zeros_like(ar)

    def lanes(first_r, second_r, first_c, second_c):
        half = jnp.concatenate([first_r, second_r, first_c, second_c], axis=-1)
        return jnp.concatenate([half, half], axis=-1)

    cos = lanes(jnp.cos(ar), jnp.cos(ar), jnp.cos(ac), jnp.cos(ac))
    s_first = lanes(-jnp.sin(ar), zero, -jnp.sin(ac), zero)
    s_second = lanes(zero, jnp.sin(ar), zero, jnp.sin(ac))
    return cos, s_first, s_second


def _prep_layer_params(stacked, layer):
    p = {name: arr[layer] for name, arr in stacked.items()}
    w_in = p["w_in"]
    dt0 = 3 * DA_W + SSM_DI + CONV_CH
    d = w_in.shape[0]
    p["w_in_p"] = jnp.concatenate(
        [w_in[:, :dt0], w_in[:, dt0 + 2 * SSM_H:], w_in[:, dt0:dt0 + 2 * SSM_H],
         jnp.zeros((d, DT_PAD - 2 * SSM_H), w_in.dtype)], axis=-1).astype(BF16)
    for name in ("w_gate", "w_br_da", "w_br_ssm", "w_br_na", "w_out", "w_s_gate", "w_s_up", "w_s_down"):
        p[name] = p[name].astype(BF16)
    ff = N_EXPERTS * EXPERT_FF
    p["w_e_gate"] = p["w_e_gate"].astype(BF16).transpose(1, 0, 2).reshape(d, ff)
    p["w_e_up"] = p["w_e_up"].astype(BF16).transpose(1, 0, 2).reshape(d, ff)
    p["w_e_down"] = p["w_e_down"].astype(BF16).reshape(ff, d)
    return p


def _trunk_layer(x3, mods, p, layer, ctx):
    b, n, d = x3.shape
    latent = ctx is not None
    rope_tabs = _rope_tables(n) if latent else None
    qkv_dtype = BF16 if latent else F32
    dq, dk, dv, sz, sxbc, nq, nk, nv, sdt, gates = _pre(x3, mods, p["g_pre1"], p["w_in_p"], p["w_gate"],
                                                        rope_tabs, qkv_dtype)
    if latent:
        ck, cv, cnk, cnv, h0f, h0b = ctx
        o_da = _da(dq, dk, dv, p["da_lambda"], p["da_subln"], layer, (ck, cv))
        o_na = _na_lat(nq, nk, nv, cnk, cnv, p["na_rpb"], layer)
        o_ssm, hf, hb = _ssd(sz, sxbc, sdt, p, (h0f, h0b))
    else:
        o_da = _da(dq, dk, dv, p["da_lambda"], p["da_subln"], layer, None)
        o_na = _na_ctx(nq, nk, nv)
        o_ssm, hf, hb = _ssd(sz, sxbc, sdt, p, None)
    x1, h2 = _post(x3, o_da, o_ssm, o_na, gates, mods, p)
    x2 = _moe(x1, h2, mods, p, n).reshape(b, n, d)
    state = None
    if not latent:
        to_state = lambda s: s.reshape(b, SSM_N, SSM_H, SSM_P).transpose(0, 2, 3, 1)
        state = (dk.reshape(b, n, DA_H, 2, DA_HD).transpose(0, 2, 1, 3, 4),
                 dv.reshape(b, n, DA_H, 2 * DA_HD).transpose(0, 2, 1, 3),
                 nk.reshape(b, n, NA_H, NA_HD).transpose(0, 2, 1, 3),
                 nv.reshape(b, n, NA_H, NA_HD).transpose(0, 2, 1, 3),
                 to_state(hf), to_state(hb))
    return x2, state


def kernel(x_prompt, x_sample, c, cache_da_k, cache_da_v, cache_na_k, cache_na_v, state_ssm_f, state_ssm_b,
           c_ctx, w_ada, b_ada, g_pre1, g_post1, g_pre2, g_post2, w_in, w_gate, da_lambda, da_subln,
           ssm_conv_w, ssm_conv_b, ssm_dt_bias, ssm_a_log, ssm_d, ssm_norm, na_rpb, w_br_da, w_br_ssm,
           w_br_na, w_out, w_router, b_router, w_e_gate, w_e_up, w_e_down, w_s_gate, w_s_up, w_s_down):
    stacked = {
        "g_pre1": g_pre1, "g_post1": g_post1, "g_pre2": g_pre2, "g_post2": g_post2, "w_in": w_in,
        "w_gate": w_gate, "da_lambda": da_lambda, "da_subln": da_subln, "ssm_conv_w": ssm_conv_w,
        "ssm_conv_b": ssm_conv_b, "ssm_dt_bias": ssm_dt_bias, "ssm_a_log": ssm_a_log, "ssm_d": ssm_d,
        "ssm_norm": ssm_norm, "na_rpb": na_rpb, "w_br_da": w_br_da, "w_br_ssm": w_br_ssm, "w_br_na": w_br_na,
        "w_out": w_out, "w_router": w_router, "b_router": b_router, "w_e_gate": w_e_gate, "w_e_up": w_e_up,
        "w_e_down": w_e_down, "w_s_gate": w_s_gate, "w_s_up": w_s_up, "w_s_down": w_s_down,
    }
    depth = w_ada.shape[0]
    nb = x_sample.shape[0]
    d = x_prompt.shape[-1]
    rows = -(-(1 + nb) // SUBLANES) * SUBLANES
    cvec = jnp.zeros((rows, d), F32).at[0].set(c_ctx).at[1:1 + nb].set(c)
    mods = _ada(cvec, w_ada, b_ada).reshape(depth, rows, 6, d)

    past = cache_da_k.shape[3]
    ck_all = cache_da_k.reshape(nb, depth, DA_H, past, 2 * DA_HD)
    pair = lambda a: a.reshape(nb, depth, NA_H // 2, 2, past, NA_HD).transpose(0, 1, 2, 4, 3, 5).reshape(
        nb, depth, NA_H // 2, past, 2 * NA_HD)
    cnk_all, cnv_all = pair(cache_na_k), pair(cache_na_v)
    st_t = lambda s: s.transpose(0, 1, 4, 2, 3).reshape(nb, depth, SSM_N, SSM_DI)
    h0f_all, h0b_all = st_t(state_ssm_f), st_t(state_ssm_b)

    xp, xs = x_prompt, x_sample
    new = ([], [], [], [], [], [])
    for layer in range(depth):
        p = _prep_layer_params(stacked, layer)
        xp, st = _trunk_layer(xp, mods[layer, 0:1], p, layer, None)
        for lst, arr in zip(new, st):
            lst.append(arr)
        ctx = (ck_all, cache_da_v, cnk_all, cnv_all, h0f_all[:, layer], h0b_all[:, layer])
        xs, _ = _trunk_layer(xs, mods[layer, 1:1 + nb], p, layer, ctx)
    return (xp, xs) + tuple(jnp.stack(lst, axis=1) for lst in new)
```

```python
import functools
import math

import numpy as np
import jax
import jax.numpy as jnp
from jax import lax
from jax.experimental import pallas as pl
from jax.experimental.pallas import tpu as pltpu

F32 = jnp.float32
BF16 = jnp.bfloat16

D_MODEL = 1024
GRID_W = 64
EPS = 1e-6
DA_H, DA_HD = 4, 64
DA_W = DA_H * 2 * DA_HD
ROPE_BASE = 10000.0
SSM_H, SSM_P, SSM_G, SSM_N = 8, 64, 2, 128
SSM_DI = SSM_H * SSM_P
SSM_CHUNK = 128
CONV_W = 5
CONV_CH = SSM_DI + 2 * SSM_G * SSM_N
NA_H, NA_HD = 8, 64
NA_W = NA_H * NA_HD
WIN_R, WIN_C = 8, 16
N_EXPERTS, TOP_K, EXPERT_FF, SHARED_FF = 32, 4, 128, 256
ROUTE_SCALE = 2.5

LANES = 128
SUBLANES = 8
VMEM_LIMIT_BYTES = 56 * 2**20

PRE_TM = 256
POST_TM = 512
MOE_TM = 512
MOE_EG = 8
DA_TQ = 512
DA_TS = 256
DA_TK = 512
NA_QB = WIN_R * GRID_W
NA_KROWS = 16
NA_KW = NA_KROWS * GRID_W
DT_PAD = LANES

OFF_DQ, OFF_DK, OFF_DV, OFF_SZ, OFF_XBC, OFF_NQ, OFF_NK, OFF_NV, OFF_DT = (
    0, 512, 1024, 1536, 2048, 3072, 3584, 4096, 4608)
IN_WP = OFF_DT + DT_PAD

_NT = (((1,), (1,)), ((), ()))
LOG2E = math.log2(math.e)
NA_QSCALE = NA_HD ** -0.5 * LOG2E


def _cparams(sem):
    return pltpu.CompilerParams(dimension_semantics=sem, vmem_limit_bytes=VMEM_LIMIT_BYTES)


def _bdot(a, b):
    return jnp.dot(a.astype(BF16), b.astype(BF16), preferred_element_type=F32)


def _bdot_nt(a, b):
    return lax.dot_general(a.astype(BF16), b.astype(BF16), _NT, preferred_element_type=F32)


def _fdot(a, b):
    return jnp.dot(a, b, preferred_element_type=F32, precision=lax.Precision.HIGHEST)


def _sigmoid(x):
    return 1.0 / (1.0 + jnp.exp(-x))


def _silu(x):
    return x * _sigmoid(x)


def _softplus(x):
    return jnp.maximum(x, 0.0) + jnp.log(1.0 + jnp.exp(-jnp.abs(x)))


def _rms(x, g):
    return x * lax.rsqrt(jnp.mean(x * x, axis=-1, keepdims=True) + EPS) * g


def _ada_kernel(c_ref, w_ref, b_ref, o_ref):
    o_ref[...] = _bdot(_silu(c_ref[...]), w_ref[...]) + b_ref[...]


def _ada(cvec, w_ada, b_ada):
    nl, d, d6 = w_ada.shape
    r = cvec.shape[0]
    tn = 1024
    return pl.pallas_call(
        _ada_kernel,
        grid=(nl, d6 // tn),
        in_specs=[pl.BlockSpec((r, d), lambda l, j: (0, 0)),
                  pl.BlockSpec((None, d, tn), lambda l, j: (l, 0, j)),
                  pl.BlockSpec((None, 1, tn), lambda l, j: (l, 0, j))],
        out_specs=pl.BlockSpec((None, r, tn), lambda l, j: (l, 0, j)),
        out_shape=jax.ShapeDtypeStruct((nl, r, d6), F32),
        compiler_params=_cparams(("arbitrary", "arbitrary")),
        name="ada",
    )(cvec, w_ada, b_ada.reshape(nl, 1, d6))


def _pre_kernel(*refs, rope):
    if rope:
        x_ref, mod_ref, g_ref, win_ref, wg_ref, cos_ref, sa_ref, sb_ref = refs[:8]
        outs = refs[8:]
    else:
        x_ref, mod_ref, g_ref, win_ref, wg_ref = refs[:5]
        outs = refs[5:]
    dq_ref, dk_ref, dv_ref, sz_ref, xbc_ref, nq_ref, nk_ref, nv_ref, dt_ref, gate_ref = outs
    x = x_ref[...]
    h = _rms(x, g_ref[...]) * (1.0 + mod_ref[1:2, :]) + mod_ref[0:1, :]
    hb = h.astype(BF16)

    def proj(off, width):
        return jnp.dot(hb, win_ref[:, off:off + width], preferred_element_type=F32)

    for off, ref in ((OFF_DQ, dq_ref), (OFF_DK, dk_ref)):
        if rope:
            cos, sa, sb = cos_ref[...], sa_ref[...], sb_ref[...]
            for hd in range(DA_H):
                u = proj(off + hd * LANES, LANES)
                u = u * cos + pltpu.roll(u, LANES - 16, 1) * sa + pltpu.roll(u, 16, 1) * sb
                ref[:, hd * LANES:(hd + 1) * LANES] = u.astype(ref.dtype)
        else:
            ref[...] = proj(off, DA_W).astype(ref.dtype)
    dv_ref[...] = proj(OFF_DV, DA_W).astype(dv_ref.dtype)
    sz_ref[...] = proj(OFF_SZ, SSM_DI)
    xbc_ref[...] = proj(OFF_XBC, CONV_CH)
    nq_ref[...] = proj(OFF_NQ, NA_W).astype(nq_ref.dtype)
    nk_ref[...] = proj(OFF_NK, NA_W).astype(nk_ref.dtype)
    nv_ref[...] = proj(OFF_NV, NA_W).astype(nv_ref.dtype)
    dt_ref[...] = proj(OFF_DT, DT_PAD)
    gate_ref[...] = _sigmoid(jnp.dot(hb, wg_ref[...], preferred_element_type=F32))


def _pre(x3, mods, g_pre1, w_in_p, w_gate, rope_tabs, qkv_dtype):
    b, n, d = x3.shape
    tm = PRE_TM
    tps = n // tm
    t = b * n
    x2 = x3.reshape(t, d)
    per_batch_mods = mods.shape[0] > 1
    mod_idx = (lambda i: (i // tps, 0, 0)) if per_batch_mods else (lambda i: (0, 0, 0))
    const2 = lambda i: (0, 0)
    row = lambda i: (i, 0)
    in_specs = [pl.BlockSpec((tm, d), row),
                pl.BlockSpec((None, 6, d), mod_idx),
                pl.BlockSpec((1, d), const2),
                pl.BlockSpec((d, IN_WP), const2, pipeline_mode=pl.Buffered(1)),
                pl.BlockSpec((d, 3 * d), const2, pipeline_mode=pl.Buffered(1))]
    args = [x2, mods, g_pre1.reshape(1, d), w_in_p, w_gate]
    rope = rope_tabs is not None
    if rope:
        in_specs += [pl.BlockSpec((tm, LANES), lambda i: (i % tps, 0))] * 3
        args += list(rope_tabs)
    widths = (DA_W, DA_W, DA_W, SSM_DI, CONV_CH, NA_W, NA_W, NA_W, DT_PAD, 3 * d)
    dtypes = (qkv_dtype, qkv_dtype, qkv_dtype, F32, F32, qkv_dtype, qkv_dtype, qkv_dtype, F32, F32)
    outs = pl.pallas_call(
        functools.partial(_pre_kernel, rope=rope),
        grid=(t // tm,),
        in_specs=in_specs,
        out_specs=[pl.BlockSpec((tm, w), row) for w in widths],
        out_shape=[jax.ShapeDtypeStruct((t, w), dt) for w, dt in zip(widths, dtypes)],
        compiler_params=_cparams(("arbitrary",)),
        name="pre_rope" if rope else "pre",
    )(*args)
    return [o.reshape(b, n, o.shape[-1]) for o in outs]


def _da_kernel(*refs, lam_init, has_ctx, tk):
    if has_ctx:
        lam_ref, sub_ref, q_ref, k_ref, v_ref, ck_ref, cv_ref, o_ref, s_ref, p_ref, sc_ref, pc_ref = refs
    else:
        lam_ref, sub_ref, q_ref, k_ref, v_ref, o_ref, s_ref, p_ref = refs
    nsub, _, _, ts, _ = s_ref.shape
    nc = k_ref.shape[0] // tk
    lf = lam_ref[...]
    lam = (jnp.exp(jnp.sum(lf[0:1] * lf[1:2], axis=-1, keepdims=True))
           - jnp.exp(jnp.sum(lf[2:3] * lf[3:4], axis=-1, keepdims=True)) + lam_init)
    lane = lax.broadcasted_iota(jnp.int32, (1, LANES), 1)

    def fold(acc, x, op):
        for j in range(x.shape[1] // LANES):
            acc = op(acc, x[:, j * LANES:(j + 1) * LANES])
        return acc

    def scores(i, t):
        q = q_ref[i * ts:(i + 1) * ts, :].astype(F32) * (DA_HD ** -0.5 * LOG2E)
        qt = jnp.where((lane >= DA_HD) == bool(t), q, 0.0).astype(BF16)
        m_acc = jnp.full((ts, LANES), -jnp.inf, F32)
        for kc in range(nc):
            s = _bdot_nt(qt, k_ref[kc * tk:(kc + 1) * tk, :])
            s_ref[i, t, kc] = s
            m_acc = fold(m_acc, s, jnp.maximum)
        if has_ctx:
            sc = _bdot_nt(qt, ck_ref[...])
            sc_ref[i, t] = sc
            m_acc = fold(m_acc, sc, jnp.maximum)
        return jnp.max(m_acc, axis=-1, keepdims=True)

    def exps(i, t, m):
        l_acc = jnp.zeros((ts, LANES), F32)
        for kc in range(nc):
            p = jnp.exp2(s_ref[i, t, kc] - m)
            p_ref[i, t, kc] = p.astype(BF16)
            l_acc = fold(l_acc, p, jnp.add)
        if has_ctx:
            pc = jnp.exp2(sc_ref[i, t] - m)
            pc_ref[i, t] = pc.astype(BF16)
            l_acc = fold(l_acc, pc, jnp.add)
        return jnp.sum(l_acc, axis=-1, keepdims=True)

    def attend(i, l0, l1):
        r = (lam * l0 / l1).astype(BF16)
        acc = jnp.zeros((ts, LANES), F32)
        for kc in range(nc):
            acc = acc + jnp.dot(p_ref[i, 0, kc] - p_ref[i, 1, kc] * r,
                                v_ref[kc * tk:(kc + 1) * tk, :].astype(BF16), preferred_element_type=F32)
        if has_ctx:
            acc = acc + jnp.dot(pc_ref[i, 0] - pc_ref[i, 1] * r, cv_ref[...].astype(BF16),
                                preferred_element_type=F32)
        o = _rms(acc * (1.0 / l0), sub_ref[...]) * (1.0 - lam_init)
        o_ref[i * ts:(i + 1) * ts, :] = o.astype(o_ref.dtype)

    m, inv = {}, {}
    for i in range(nsub + 1):
        if i < nsub:
            m[i, 0] = scores(i, 0)
        if i > 0:
            inv[i - 1, 1] = exps(i - 1, 1, m[i - 1, 1])
        if i < nsub:
            m[i, 1] = scores(i, 1)
        if i > 0:
            attend(i - 1, inv[i - 1, 0], inv[i - 1, 1])
        if i < nsub:
            inv[i, 0] = exps(i, 0, m[i, 0])


def _da(dq, dk, dv, lam_p, subln, layer, ctx_kv):
    b, n, _ = dq.shape
    lam_init = 0.8 - 0.6 * math.exp(-0.3 * layer)
    tq = min(DA_TQ, n)
    ts = min(DA_TS, tq)
    tk = min(DA_TK, n)
    has_ctx = ctx_kv is not None
    qmap = lambda bi, h, qi: (bi, qi, h)
    kvmap = lambda bi, h, qi: (bi, 0, h)
    in_specs = [pl.BlockSpec((4, DA_HD), lambda bi, h, qi: (0, 0)),
                pl.BlockSpec((1, 2 * DA_HD), lambda bi, h, qi: (0, 0)),
                pl.BlockSpec((None, tq, LANES), qmap),
                pl.BlockSpec((None, n, LANES), kvmap),
                pl.BlockSpec((None, n, LANES), kvmap)]
    args = [lam_p, subln.reshape(1, 2 * DA_HD), dq, dk, dv]
    scratch = [pltpu.VMEM((tq // ts, 2, n // tk, ts, tk), F32), pltpu.VMEM((tq // ts, 2, n // tk, ts, tk), BF16)]
    if has_ctx:
        ck, cv = ctx_kv
        past = ck.shape[3]
        cmap = lambda bi, h, qi: (bi, layer, h, 0, 0)
        in_specs += [pl.BlockSpec((None, None, None, past, LANES), cmap)] * 2
        args += [ck, cv]
        scratch += [pltpu.VMEM((tq // ts, 2, ts, past), F32), pltpu.VMEM((tq // ts, 2, ts, past), BF16)]
    return pl.pallas_call(
        functools.partial(_da_kernel, lam_init=lam_init, has_ctx=has_ctx, tk=tk),
        grid=(b, DA_H, n // tq),
        in_specs=in_specs,
        out_specs=pl.BlockSpec((None, tq, LANES), qmap),
        out_shape=jax.ShapeDtypeStruct((b, n, DA_W), F32),
        scratch_shapes=scratch,
        compiler_params=_cparams(("arbitrary", "arbitrary", "arbitrary")),
        name="da_lat" if has_ctx else "da_ctx",
    )(*args)


def _pair_attend(q, sources):
    lane = lax.broadcasted_iota(jnp.int32, (1, LANES), 1)
    outs = []
    for j in (0, 1):
        qj = jnp.where((lane >= NA_HD) == bool(j), q, 0.0).astype(BF16)
        ss = []
        for k, _, bias in sources:
            s = _bdot_nt(qj, k)
            if bias is not None:
                s = s + bias[j]
            ss.append(s)
        m = ss[0].max(axis=-1, keepdims=True)
        for s in ss[1:]:
            m = jnp.maximum(m, s.max(axis=-1, keepdims=True))
        l = 0.0
        o = 0.0
        for s, (_, v, _) in zip(ss, sources):
            p = jnp.exp2(s - m)
            l = l + p.sum(axis=-1, keepdims=True)
            o = o + _bdot(p, v)
        outs.append(o * (1.0 / l))
    return jnp.where(lane >= NA_HD, outs[1], outs[0])


def _na_lat_kernel(q_ref, k_ref, v_ref, ck_ref, cv_ref, b_ref, o_ref, *, n_rows):
    rb = pl.program_id(2)
    row0 = jnp.clip(rb * WIN_R - WIN_R // 2, 0, n_rows - NA_KROWS)
    start = pl.multiple_of(row0 * GRID_W, GRID_W)
    kw = k_ref[pl.ds(start, NA_KW), :]
    vw = v_ref[pl.ds(start, NA_KW), :]
    q = q_ref[...].astype(F32) * NA_QSCALE
    o_ref[...] = _pair_attend(q, [(kw, vw, b_ref), (ck_ref[...], cv_ref[...], None)])


def _na_ctx_kernel(q_ref, k_ref, v_ref, o_ref):
    q = q_ref[...].astype(F32) * NA_QSCALE
    o_ref[...] = _pair_attend(q, [(k_ref[...], v_ref[...], None)])


def _na_bias_tables(rpb, n_rows):
    nrb = n_rows // WIN_R
    nh = rpb.shape[0]
    pad = GRID_W - WIN_C
    rpad = jnp.pad(rpb.astype(F32), ((0, 0), (0, 0), (pad, pad)))
    toep = jnp.stack([rpad[:, :, GRID_W - 1 - qc:2 * GRID_W - 1 - qc] for qc in range(GRID_W)], axis=2)
    cols = np.arange(GRID_W)
    s_c = np.clip(cols - WIN_C // 2, 0, GRID_W - WIN_C)[:, None]
    ok_c = (cols[None, :] >= s_c) & (cols[None, :] < s_c + WIN_C)
    toep = jnp.where(jnp.asarray(ok_c), toep, -jnp.inf)
    masked = jnp.full((nh, GRID_W, GRID_W), -jnp.inf, F32)
    tabs = []
    for rb in (0, 1, nrb - 1):
        row0 = min(max(rb * WIN_R - WIN_R // 2, 0), n_rows - NA_KROWS)
        q_rows = rb * WIN_R + np.arange(WIN_R)
        k_rows = row0 + np.arange(NA_KROWS)
        s_r = np.clip(q_rows - WIN_R // 2, 0, n_rows - WIN_R)[:, None]
        ok_r = (k_rows[None, :] >= s_r) & (k_rows[None, :] < s_r + WIN_R)
        off_r = np.clip(k_rows[None, :] - q_rows[:, None] + WIN_R - 1, 0, 2 * WIN_R - 2)
        rows = [jnp.concatenate([toep[:, off_r[qr, kr]] if ok_r[qr, kr] else masked for kr in range(NA_KROWS)],
                                axis=-1) for qr in range(WIN_R)]
        tabs.append(jnp.concatenate(rows, axis=-2) * LOG2E)
    return jnp.stack(tabs, axis=1)


def _na_lat(nq, nk, nv, ck, cv, rpb, layer):
    b, n, _ = nq.shape
    n_rows = n // GRID_W
    assert n_rows >= NA_KROWS and n_rows % WIN_R == 0
    nrb = n_rows // WIN_R
    past = ck.shape[3]
    bias = _na_bias_tables(rpb, n_rows)
    kvmap = lambda bi, hp, rb: (bi, 0, hp)
    cmap = lambda bi, hp, rb: (bi, layer, hp, 0, 0)
    variant = lambda bi, hp, rb: (hp, jnp.where(rb == 0, 0, jnp.where(rb == nrb - 1, 2, 1)), 0, 0)
    return pl.pallas_call(
        functools.partial(_na_lat_kernel, n_rows=n_rows),
        grid=(b, NA_H // 2, nrb),
        in_specs=[pl.BlockSpec((None, NA_QB, LANES), lambda bi, hp, rb: (bi, rb, hp)),
                  pl.BlockSpec((None, n, LANES), kvmap),
                  pl.BlockSpec((None, n, LANES), kvmap),
                  pl.BlockSpec((None, None, None, past, LANES), cmap),
                  pl.BlockSpec((None, None, None, past, LANES), cmap),
                  pl.BlockSpec((2, None, NA_QB, NA_KW), variant)],
        out_specs=pl.BlockSpec((None, NA_QB, LANES), lambda bi, hp, rb: (bi, rb, hp)),
        out_shape=jax.ShapeDtypeStruct((b, n, NA_W), F32),
        compiler_params=_cparams(("arbitrary", "arbitrary", "arbitrary")),
        name="na_lat",
    )(nq, nk, nv, ck, cv, bias)


def _na_ctx(nq, nk, nv):
    b, n, _ = nq.shape
    spec = pl.BlockSpec((None, n, LANES), lambda bi, hp: (bi, 0, hp))
    return pl.pallas_call(
        _na_ctx_kernel,
        grid=(b, NA_H // 2),
        in_specs=[spec, spec, spec],
        out_specs=spec,
        out_shape=jax.ShapeDtypeStruct((b, n, NA_W), F32),
        compiler_params=_cparams(("arbitrary", "arbitrary")),
        name="na_ctx",
    )(nq, nk, nv)


def _ssd_kernel(*refs, nc, has_h0):
    if has_h0:
        (xm_ref, xp_ref, xn_ref, z_ref, dt_ref, dtt_ref, cw_ref, cb_ref, dtb_ref, dtbt_ref, alogt_ref,
         alogx_ref, dexp_ref, nrm_ref, h0f_ref, h0b_ref, y_ref, hf_ref, hb_ref, xpad_ref, st_ref, yacc_ref) = refs
    else:
        (xm_ref, xp_ref, xn_ref, z_ref, dt_ref, dtt_ref, cw_ref, cb_ref, dtb_ref, dtbt_ref, alogt_ref,
         alogx_ref, dexp_ref, nrm_ref, y_ref, hf_ref, hb_ref, xpad_ref, st_ref, yacc_ref) = refs
    j = pl.program_id(1)
    t = SSM_CHUNK
    gw = SSM_DI // SSM_G
    p_shift = SSM_P.bit_length() - 1
    bwd = j >= nc
    c = jnp.where(bwd, 2 * nc - 1 - j, j)

    @pl.when(j == 0)
    def _():
        st_ref[...] = h0f_ref[...] if has_h0 else jnp.zeros_like(st_ref)

    @pl.when(j == nc)
    def _():
        st_ref[...] = h0b_ref[...] if has_h0 else jnp.zeros_like(st_ref)

    xpad_ref[0:SUBLANES, :] = jnp.where(c == 0, 0.0, xp_ref[...])
    xpad_ref[SUBLANES:SUBLANES + t, :] = xm_ref[...]
    xpad_ref[SUBLANES + t:, :] = jnp.where(c == nc - 1, 0.0, xn_ref[...])
    conv = cb_ref[...]
    for k in range(CONV_W):
        o = SUBLANES - CONV_W // 2 + k
        conv = conv + cw_ref[k:k + 1, :] * xpad_ref[o:o + t, :]
    act = _silu(conv)
    xs = act[:, :SSM_DI]
    bm = act[:, SSM_DI:SSM_DI + SSM_G * SSM_N]
    cm = act[:, SSM_DI + SSM_G * SSM_N:]

    li = lax.broadcasted_iota(jnp.int32, (t, t), 0)
    si = lax.broadcasted_iota(jnp.int32, (t, t), 1)
    keep = jnp.where(bwd, li - si, si - li) <= 0
    tri = keep.astype(F32)
    trit = (jnp.where(bwd, si - li, li - si) <= 0).astype(F32)
    dirv = bwd.astype(jnp.int32)
    r_i = lax.broadcasted_iota(jnp.int32, (LANES, SSM_DI), 0)
    c_i = lax.broadcasted_iota(jnp.int32, (LANES, SSM_DI), 1)
    expand = (r_i == dirv * SSM_H + (c_i >> p_shift)).astype(F32)
    dtv = _softplus(dt_ref[...] + dtb_ref[...])
    dtx = _fdot(dtv, expand)
    a_exp = -jnp.exp(jnp.where(bwd, alogx_ref[1:2, :], alogx_ref[0:1, :]))
    acum = _fdot(tri, dtx * a_exp)
    tot = jnp.where(bwd, acum[0:1, :], acum[t - 1:t, :])
    e_in = jnp.exp(acum)
    e_out = jnp.exp(tot - acum)
    xdt = xs * dtx
    xdt_b = xdt.astype(BF16)
    xout_b = (xdt * e_out).astype(BF16)

    at = -jnp.exp(alogt_ref[...]) * _softplus(dtt_ref[...] + dtbt_ref[...])
    acum_t = _fdot(at, trit)

    lane_g = lax.broadcasted_iota(jnp.int32, (1, gw), 1)
    y_parts = []
    for g in range(SSM_G):
        cg = cm[:, g * SSM_N:(g + 1) * SSM_N].astype(BF16)
        bg = bm[:, g * SSM_N:(g + 1) * SSM_N]
        gmat = _bdot_nt(cg, bg)
        st_g = st_ref[:, g * gw:(g + 1) * gw]
        y_g = _bdot(cg, st_g) * e_in[:, g * gw:(g + 1) * gw]
        for hh in range(SSM_H // SSM_G):
            h = g * (SSM_H // SSM_G) + hh
            col = acum[:, h * SSM_P:h * SSM_P + 1]
            row = jnp.where(bwd, acum_t[SSM_H + h:SSM_H + h + 1, :], acum_t[h:h + 1, :])
            lmat = jnp.exp(jnp.where(keep, col - row, -jnp.inf))
            yh = _bdot(gmat * lmat, xdt_b[:, g * gw:(g + 1) * gw])
            y_g = y_g + jnp.where((lane_g >> p_shift) == hh, yh, 0.0)
        y_parts.append(y_g)
        st_new = st_g * jnp.exp(tot[:, g * gw:(g + 1) * gw]) + _bdot(bg.T, xout_b[:, g * gw:(g + 1) * gw])
        st_ref[:, g * gw:(g + 1) * gw] = st_new
    y_dir = jnp.concatenate(y_parts, axis=-1)

    row0 = pl.multiple_of(c * t, t)

    @pl.when(jnp.logical_not(bwd))
    def _():
        yacc_ref[pl.ds(row0, t), :] = y_dir + xs * dexp_ref[...]

    @pl.when(j == nc - 1)
    def _():
        hf_ref[...] = st_ref[...]

    @pl.when(bwd)
    def _():
        y = yacc_ref[pl.ds(row0, t), :] + y_dir
        y = y * _silu(z_ref[...])
        parts = [_rms(y[:, g * gw:(g + 1) * gw], nrm_ref[:, g * gw:(g + 1) * gw]) for g in range(SSM_G)]
        y_ref[...] = jnp.concatenate(parts, axis=-1)

    @pl.when(j == 2 * nc - 1)
    def _():
        hb_ref[...] = st_ref[...]


def _ssd(sz, sxbc, sdt, p, h0):
    b, n, _ = sz.shape
    t = SSM_CHUNK
    nc = n // t
    nb8 = n // SUBLANES
    tb = t // SUBLANES
    has_h0 = h0 is not None
    sdt_t = jnp.swapaxes(sdt[:, :, :2 * SSM_H], 1, 2)
    cidx = lambda j: jnp.where(j >= nc, 2 * nc - 1 - j, j)
    zidx = lambda j: jnp.where(j >= nc, 2 * nc - 1 - j, nc - 1)
    const2 = lambda bi, j: (0, 0)
    cw = jnp.zeros((SUBLANES, CONV_CH), F32).at[:CONV_W].set(p["ssm_conv_w"])
    dtb = jnp.zeros((1, DT_PAD), F32).at[0, :2 * SSM_H].set(p["ssm_dt_bias"].reshape(-1))
    dtbt = jnp.broadcast_to(p["ssm_dt_bias"].reshape(2 * SSM_H, 1), (2 * SSM_H, t))
    alogt = jnp.broadcast_to(p["ssm_a_log"].reshape(2 * SSM_H, 1), (2 * SSM_H, t))
    alogx = jnp.repeat(p["ssm_a_log"], SSM_P, axis=-1)
    dexp = jnp.repeat(p["ssm_d"], SSM_P).reshape(1, SSM_DI)
    in_specs = [pl.BlockSpec((None, t, CONV_CH), lambda bi, j: (bi, cidx(j), 0)),
                pl.BlockSpec((None, SUBLANES, CONV_CH), lambda bi, j: (bi, jnp.maximum(cidx(j) * tb - 1, 0), 0)),
                pl.BlockSpec((None, SUBLANES, CONV_CH),
                             lambda bi, j: (bi, jnp.minimum((cidx(j) + 1) * tb, nb8 - 1), 0)),
                pl.BlockSpec((None, t, SSM_DI), lambda bi, j: (bi, zidx(j), 0)),
                pl.BlockSpec((None, t, DT_PAD), lambda bi, j: (bi, cidx(j), 0)),
                pl.BlockSpec((None, 2 * SSM_H, t), lambda bi, j: (bi, 0, cidx(j))),
                pl.BlockSpec((SUBLANES, CONV_CH), const2),
                pl.BlockSpec((1, CONV_CH), const2),
                pl.BlockSpec((1, DT_PAD), const2),
                pl.BlockSpec((2 * SSM_H, t), const2),
                pl.BlockSpec((2 * SSM_H, t), const2),
                pl.BlockSpec((2, SSM_DI), const2),
                pl.BlockSpec((1, SSM_DI), const2),
                pl.BlockSpec((1, SSM_DI), const2)]
    args = [sxbc, sxbc, sxbc, sz, sdt, sdt_t, cw, p["ssm_conv_b"].reshape(1, CONV_CH), dtb, dtbt,
            alogt, alogx, dexp, p["ssm_norm"].reshape(1, SSM_DI)]
    st_spec = pl.BlockSpec((None, SSM_N, SSM_DI), lambda bi, j: (bi, 0, 0))
    if has_h0:
        in_specs += [st_spec, st_spec]
        args += list(h0)
    y, hf, hb = pl.pallas_call(
        functools.partial(_ssd_kernel, nc=nc, has_h0=has_h0),
        grid=(b, 2 * nc),
        in_specs=in_specs,
        out_specs=[pl.BlockSpec((None, t, SSM_DI), lambda bi, j: (bi, zidx(j), 0)), st_spec, st_spec],
        out_shape=[jax.ShapeDtypeStruct((b, n, SSM_DI), F32),
                   jax.ShapeDtypeStruct((b, SSM_N, SSM_DI), F32),
                   jax.ShapeDtypeStruct((b, SSM_N, SSM_DI), F32)],
        scratch_shapes=[pltpu.VMEM((t + 2 * SUBLANES, CONV_CH), F32),
                        pltpu.VMEM((SSM_N, SSM_DI), F32),
                        pltpu.VMEM((n, SSM_DI), F32)],
        compiler_params=_cparams(("arbitrary", "arbitrary")),
        name="ssd_lat" if has_h0 else "ssd_ctx",
    )(*args)
    return y, hf, hb


def _post_kernel(x_ref, oda_ref, ossm_ref, ona_ref, gate_ref, mod_ref, wda_ref, wssm_ref, wna_ref, wout_ref,
                 gpost_ref, gpre_ref, x1_ref, h2_ref):
    d = x_ref.shape[1]
    mix = (gate_ref[:, 0:d] * _bdot(oda_ref[...], wda_ref[...])
           + gate_ref[:, d:2 * d] * _bdot(ossm_ref[...], wssm_ref[...])
           + gate_ref[:, 2 * d:3 * d] * _bdot(ona_ref[...], wna_ref[...]))
    mix = _bdot(mix, wout_ref[...])
    x1 = x_ref[...] + mod_ref[2:3, :] * _rms(mix, gpost_ref[...])
    x1_ref[...] = x1
    h2_ref[...] = _rms(x1, gpre_ref[...]) * (1.0 + mod_ref[4:5, :]) + mod_ref[3:4, :]


def _post(x3, o_da, o_ssm, o_na, gates, mods, p):
    b, n, d = x3.shape
    tm = min(POST_TM, n)
    tps = n // tm
    t = b * n
    per_batch_mods = mods.shape[0] > 1
    mod_idx = (lambda i: (i // tps, 0, 0)) if per_batch_mods else (lambda i: (0, 0, 0))
    row = lambda i: (i, 0)
    const2 = lambda i: (0, 0)
    flat = lambda a: a.reshape(t, a.shape[-1])
    wspec = lambda k: pl.BlockSpec((k, d), const2)
    x1, h2 = pl.pallas_call(
        _post_kernel,
        grid=(t // tm,),
        in_specs=[pl.BlockSpec((tm, d), row), pl.BlockSpec((tm, DA_W), row), pl.BlockSpec((tm, SSM_DI), row),
                  pl.BlockSpec((tm, NA_W), row), pl.BlockSpec((tm, 3 * d), row),
                  pl.BlockSpec((None, 6, d), mod_idx),
                  wspec(DA_W), wspec(SSM_DI), wspec(NA_W), wspec(d),
                  pl.BlockSpec((1, d), const2), pl.BlockSpec((1, d), const2)],
        out_specs=[pl.BlockSpec((tm, d), row), pl.BlockSpec((tm, d), row)],
        out_shape=[jax.ShapeDtypeStruct((t, d), F32), jax.ShapeDtypeStruct((t, d), F32)],
        compiler_params=_cparams(("arbitrary",)),
        name="post",
    )(flat(x3), flat(o_da), flat(o_ssm), flat(o_na), flat(gates), mods,
      p["w_br_da"], p["w_br_ssm"], p["w_br_na"], p["w_out"],
      p["g_post1"].reshape(1, d), p["g_pre2"].reshape(1, d))
    return x1, h2


def _moe_kernel(x1_ref, h2_ref, mod_ref, wr_ref, br_ref, wsg_ref, wsu_ref, wsd_ref, weg_ref, weu_ref, wed_ref,
                gpost_ref, o_ref, acc_ref, gate_ref):
    e = pl.program_id(1)
    tm = h2_ref.shape[0]
    h2 = h2_ref[...]
    hb = h2.astype(BF16)

    @pl.when(e == 0)
    def _():
        s = _sigmoid(_fdot(h2, wr_ref[...]))
        work = s + br_ref[...]
        col = lax.broadcasted_iota(jnp.int32, (tm, LANES), 1).astype(F32)
        picked = jnp.zeros((tm, LANES), F32)
        for _ in range(TOP_K):
            m = jnp.max(work, axis=-1, keepdims=True)
            first = jnp.min(jnp.where(work == m, col, float(LANES)), axis=-1, keepdims=True)
            hit = col == first
            picked = jnp.where(hit, 1.0, picked)
            work = jnp.where(hit, -jnp.inf, work)
        sel = picked * s
        g = sel / jnp.sum(sel, axis=-1, keepdims=True) * ROUTE_SCALE
        g_hi = g.astype(BF16).astype(F32)
        r1 = g - g_hi
        g_mid = r1.astype(BF16).astype(F32)
        g_lo = r1 - g_mid
        gate_ref[...] = g_hi + pltpu.roll(g_mid, N_EXPERTS, 1) + pltpu.roll(g_lo, 2 * N_EXPERTS, 1)
        acc_ref[...] = _bdot(_silu(_bdot(hb, wsg_ref[...])) * _bdot(hb, wsu_ref[...]), wsd_ref[...])

    width = MOE_EG * EXPERT_FF
    r_i = lax.broadcasted_iota(jnp.int32, (LANES, width), 0)
    c_i = lax.broadcasted_iota(jnp.int32, (LANES, width), 1)
    ff_shift = EXPERT_FF.bit_length() - 1
    onehot = jnp.where((r_i < 3 * N_EXPERTS) & ((r_i & (N_EXPERTS - 1)) == e * MOE_EG + (c_i >> ff_shift)),
                       1.0, 0.0).astype(BF16)
    g_exp = jnp.dot(gate_ref[...].astype(BF16), onehot, preferred_element_type=F32)
    act = _silu(_bdot(hb, weg_ref[...])) * _bdot(hb, weu_ref[...]) * g_exp
    acc_ref[...] += _bdot(act, wed_ref[...])

    @pl.when(e == pl.num_programs(1) - 1)
    def _():
        o_ref[...] = x1_ref[...] + mod_ref[5:6, :] * _rms(acc_ref[...], gpost_ref[...])


def _moe(x1, h2, mods, p, n):
    t, d = x1.shape
    tm = MOE_TM
    per_batch_mods = mods.shape[0] > 1
    if per_batch_mods:
        assert n % tm == 0
    tps = max(n // tm, 1)
    mod_idx = (lambda i, e: (i // tps, 0, 0)) if per_batch_mods else (lambda i, e: (0, 0, 0))
    row = lambda i, e: (i, 0)
    const2 = lambda i, e: (0, 0)
    width = MOE_EG * EXPERT_FF
    w_router = jnp.zeros((d, LANES), F32).at[:, :N_EXPERTS].set(p["w_router"])
    b_router = jnp.full((1, LANES), -jnp.inf, F32).at[0, :N_EXPERTS].set(p["b_router"])
    return pl.pallas_call(
        _moe_kernel,
        grid=(t // tm, N_EXPERTS // MOE_EG),
        in_specs=[pl.BlockSpec((tm, d), row), pl.BlockSpec((tm, d), row),
                  pl.BlockSpec((None, 6, d), mod_idx),
                  pl.BlockSpec((d, LANES), const2), pl.BlockSpec((1, LANES), const2),
                  pl.BlockSpec((d, SHARED_FF), const2), pl.BlockSpec((d, SHARED_FF), const2),
                  pl.BlockSpec((SHARED_FF, d), const2),
                  pl.BlockSpec((d, width), lambda i, e: (0, e)),
                  pl.BlockSpec((d, width), lambda i, e: (0, e)),
                  pl.BlockSpec((width, d), lambda i, e: (e, 0)),
                  pl.BlockSpec((1, d), const2)],
        out_specs=pl.BlockSpec((tm, d), row),
        out_shape=jax.ShapeDtypeStruct((t, d), F32),
        scratch_shapes=[pltpu.VMEM((tm, d), F32), pltpu.VMEM((tm, LANES), F32)],
        compiler_params=_cparams(("arbitrary", "arbitrary")),
        name="moe",
    )(x1, h2, mods, w_router, b_router,
      p["w_s_gate"], p["w_s_up"], p["w_s_down"], p["w_e_gate"], p["w_e_up"], p["w_e_down"],
      p["g_post2"].reshape(1, d))


def _rope_tables(n):
    pos = jnp.arange(n)
    rows = (pos // GRID_W).astype(F32)
    cols = (pos % GRID_W).astype(F32)
    n_freq = DA_HD // 4
    inv = ROPE_BASE ** (-jnp.arange(n_freq, dtype=F32) / n_freq)
    ar, ac = rows[:, None] * inv, cols[:, None] * inv
    zero = jnp.zeros_like(ar)

    def lanes(first_r, second_r, first_c, second_c):
        half = jnp.concatenate([first_r, second_r, first_c, second_c], axis=-1)
        return jnp.concatenate([half, half], axis=-1)

    cos = lanes(jnp.cos(ar), jnp.cos(ar), jnp.cos(ac), jnp.cos(ac))
    s_first = lanes(-jnp.sin(ar), zero, -jnp.sin(ac), zero)
    s_second = lanes(zero, jnp.sin(ar), zero, jnp.sin(ac))
    return cos, s_first, s_second


def _prep_layer_params(stacked, layer):
    p = {name: arr[layer] for name, arr in stacked.items()}
    w_in = p["w_in"]
    dt0 = 3 * DA_W + SSM_DI + CONV_CH
    d = w_in.shape[0]
    p["w_in_p"] = jnp.concatenate(
        [w_in[:, :dt0], w_in[:, dt0 + 2 * SSM_H:], w_in[:, dt0:dt0 + 2 * SSM_H],
         jnp.zeros((d, DT_PAD - 2 * SSM_H), w_in.dtype)], axis=-1).astype(BF16)
    for name in ("w_gate", "w_br_da", "w_br_ssm", "w_br_na", "w_out", "w_s_gate", "w_s_up", "w_s_down"):
        p[name] = p[name].astype(BF16)
    ff = N_EXPERTS * EXPERT_FF
    p["w_e_gate"] = p["w_e_gate"].astype(BF16).transpose(1, 0, 2).reshape(d, ff)
    p["w_e_up"] = p["w_e_up"].astype(BF16).transpose(1, 0, 2).reshape(d, ff)
    p["w_e_down"] = p["w_e_down"].astype(BF16).reshape(ff, d)
    return p


def _trunk_layer(x3, mods, p, layer, ctx):
    b, n, d = x3.shape
    latent = ctx is not None
    rope_tabs = _rope_tables(n) if latent else None
    qkv_dtype = BF16 if latent else F32
    dq, dk, dv, sz, sxbc, nq, nk, nv, sdt, gates = _pre(x3, mods, p["g_pre1"], p["w_in_p"], p["w_gate"],
                                                        rope_tabs, qkv_dtype)
    if latent:
        ck, cv, cnk, cnv, h0f, h0b = ctx
        o_da = _da(dq, dk, dv, p["da_lambda"], p["da_subln"], layer, (ck, cv))
        o_na = _na_lat(nq, nk, nv, cnk, cnv, p["na_rpb"], layer)
        o_ssm, hf, hb = _ssd(sz, sxbc, sdt, p, (h0f, h0b))
    else:
        o_da = _da(dq, dk, dv, p["da_lambda"], p["da_subln"], layer, None)
        o_na = _na_ctx(nq, nk, nv)
        o_ssm, hf, hb = _ssd(sz, sxbc, sdt, p, None)
    x1, h2 = _post(x3, o_da, o_ssm, o_na, gates, mods, p)
    x2 = _moe(x1, h2, mods, p, n).reshape(b, n, d)
    state = None
    if not latent:
        to_state = lambda s: s.reshape(b, SSM_N, SSM_H, SSM_P).transpose(0, 2, 3, 1)
        state = (dk.reshape(b, n, DA_H, 2, DA_HD).transpose(0, 2, 1, 3, 4),
                 dv.reshape(b, n, DA_H, 2 * DA_HD).transpose(0, 2, 1, 3),
                 nk.reshape(b, n, NA_H, NA_HD).transpose(0, 2, 1, 3),
                 nv.reshape(b, n, NA_H, NA_HD).transpose(0, 2, 1, 3),
                 to_state(hf), to_state(hb))
    return x2, state


def kernel(x_prompt, x_sample, c, cache_da_k, cache_da_v, cache_na_k, cache_na_v, state_ssm_f, state_ssm_b,
           c_ctx, w_ada, b_ada, g_pre1, g_post1, g_pre2, g_post2, w_in, w_gate, da_lambda, da_subln,
           ssm_conv_w, ssm_conv_b, ssm_dt_bias, ssm_a_log, ssm_d, ssm_norm, na_rpb, w_br_da, w_br_ssm,
           w_br_na, w_out, w_router, b_router, w_e_gate, w_e_up, w_e_down, w_s_gate, w_s_up, w_s_down):
    stacked = {
        "g_pre1": g_pre1, "g_post1": g_post1, "g_pre2": g_pre2, "g_post2": g_post2, "w_in": w_in,
        "w_gate": w_gate, "da_lambda": da_lambda, "da_subln": da_subln, "ssm_conv_w": ssm_conv_w,
        "ssm_conv_b": ssm_conv_b, "ssm_dt_bias": ssm_dt_bias, "ssm_a_log": ssm_a_log, "ssm_d": ssm_d,
        "ssm_norm": ssm_norm, "na_rpb": na_rpb, "w_br_da": w_br_da, "w_br_ssm": w_br_ssm, "w_br_na": w_br_na,
        "w_out": w_out, "w_router": w_router, "b_router": b_router, "w_e_gate": w_e_gate, "w_e_up": w_e_up,
        "w_e_down": w_e_down, "w_s_gate": w_s_gate, "w_s_up": w_s_up, "w_s_down": w_s_down,
    }
    depth = w_ada.shape[0]
    nb = x_sample.shape[0]
    d = x_prompt.shape[-1]
    rows = -(-(1 + nb) // SUBLANES) * SUBLANES
    cvec = jnp.zeros((rows, d), F32).at[0].set(c_ctx).at[1:1 + nb].set(c)
    mods = _ada(cvec, w_ada, b_ada).reshape(depth, rows, 6, d)

    past = cache_da_k.shape[3]
    ck_all = cache_da_k.reshape(nb, depth, DA_H, past, 2 * DA_HD)
    pair = lambda a: a.reshape(nb, depth, NA_H // 2, 2, past, NA_HD).transpose(0, 1, 2, 4, 3, 5).reshape(
        nb, depth, NA_H // 2, past, 2 * NA_HD)
    cnk_all, cnv_all = pair(cache_na_k), pair(cache_na_v)
    st_t = lambda s: s.transpose(0, 1, 4, 2, 3).reshape(nb, depth, SSM_N, SSM_DI)
    h0f_all, h0b_all = st_t(state_ssm_f), st_t(state_ssm_b)

    xp, xs = x_prompt, x_sample
    new = ([], [], [], [], [], [])
    for layer in range(depth):
        p = _prep_layer_params(stacked, layer)
        xp, st = _trunk_layer(xp, mods[layer, 0:1], p, layer, None)
        for lst, arr in zip(new, st):
            lst.append(arr)
        ctx = (ck_all, cache_da_v, cnk_all, cnv_all, h0f_all[:, layer], h0b_all[:, layer])
        xs, _ = _trunk_layer(xs, mods[layer, 1:1 + nb], p, layer, ctx)
    return (xp, xs) + tuple(jnp.stack(lst, axis=1) for lst in new)
```

```python
import functools
import math

import numpy as np
import jax
import jax.numpy as jnp
from jax import lax
from jax.experimental import pallas as pl
from jax.experimental.pallas import tpu as pltpu

F32 = jnp.float32
BF16 = jnp.bfloat16

D_MODEL = 1024
GRID_W = 64
EPS = 1e-6
DA_H, DA_HD = 4, 64
DA_W = DA_H * 2 * DA_HD
ROPE_BASE = 10000.0
SSM_H, SSM_P, SSM_G, SSM_N = 8, 64, 2, 128
SSM_DI = SSM_H * SSM_P
SSM_CHUNK = 128
CONV_W = 5
CONV_CH = SSM_DI + 2 * SSM_G * SSM_N
NA_H, NA_HD = 8, 64
NA_W = NA_H * NA_HD
WIN_R, WIN_C = 8, 16
N_EXPERTS, TOP_K, EXPERT_FF, SHARED_FF = 32, 4, 128, 256
ROUTE_SCALE = 2.5

LANES = 128
SUBLANES = 8
VMEM_LIMIT_BYTES = 56 * 2**20

PRE_TM = 256
POST_TM = 512
POST_TS = 512
MOE_TM = 512
MOE_EG = 16
DA_TQ = 512
DA_TS = 256
SSD_CPS = 4
DA_TK = 512
NA_QB = WIN_R * GRID_W
NA_KROWS = 16
NA_KW = NA_KROWS * GRID_W
DT_PAD = LANES

OFF_DQ, OFF_DK, OFF_DV, OFF_SZ, OFF_XBC, OFF_NQ, OFF_NK, OFF_NV, OFF_DT = (
    0, 512, 1024, 1536, 2048, 3072, 3584, 4096, 4608)
IN_WP = OFF_DT + DT_PAD

_NT = (((1,), (1,)), ((), ()))
LOG2E = math.log2(math.e)
NA_QSCALE = NA_HD ** -0.5 * LOG2E


def _cparams(sem):
    return pltpu.CompilerParams(dimension_semantics=sem, vmem_limit_bytes=VMEM_LIMIT_BYTES)


def _bdot(a, b):
    return jnp.dot(a.astype(BF16), b.astype(BF16), preferred_element_type=F32)


def _bdot_nt(a, b):
    return lax.dot_general(a.astype(BF16), b.astype(BF16), _NT, preferred_element_type=F32)


def _fdot(a, b):
    return jnp.dot(a, b, preferred_element_type=F32, precision=lax.Precision.HIGHEST)


def _sigmoid(x):
    return 1.0 / (1.0 + jnp.exp(-x))


def _silu(x):
    return x * _sigmoid(x)


def _softplus(x):
    return jnp.maximum(x, 0.0) + jnp.log(1.0 + jnp.exp(-jnp.abs(x)))


def _rms(x, g):
    return x * lax.rsqrt(jnp.mean(x * x, axis=-1, keepdims=True) + EPS) * g


def _ada_kernel(c_ref, w_ref, b_ref, o_ref):
    o_ref[...] = _bdot(_silu(c_ref[...]), w_ref[...]) + b_ref[...]


def _ada(cvec, w_ada, b_ada):
    nl, d, d6 = w_ada.shape
    r = cvec.shape[0]
    tn = 1024
    return pl.pallas_call(
        _ada_kernel,
        grid=(nl, d6 // tn),
        in_specs=[pl.BlockSpec((r, d), lambda l, j: (0, 0)),
                  pl.BlockSpec((None, d, tn), lambda l, j: (l, 0, j)),
                  pl.BlockSpec((None, 1, tn), lambda l, j: (l, 0, j))],
        out_specs=pl.BlockSpec((None, r, tn), lambda l, j: (l, 0, j)),
        out_shape=jax.ShapeDtypeStruct((nl, r, d6), F32),
        compiler_params=_cparams(("arbitrary", "arbitrary")),
        name="ada",
    )(cvec, w_ada, b_ada.reshape(nl, 1, d6))


def _pre_kernel(*refs, rope):
    if rope:
        x_ref, mod_ref, g_ref, win_ref, wg_ref, cos_ref, sa_ref, sb_ref = refs[:8]
        outs = refs[8:]
    else:
        x_ref, mod_ref, g_ref, win_ref, wg_ref = refs[:5]
        outs = refs[5:]
    dq_ref, dk_ref, dv_ref, sz_ref, xbc_ref, nq_ref, nk_ref, nv_ref, dt_ref, gate_ref = outs
    x = x_ref[...]
    h = _rms(x, g_ref[...]) * (1.0 + mod_ref[1:2, :]) + mod_ref[0:1, :]
    hb = h.astype(BF16)

    def proj(off, width):
        return jnp.dot(hb, win_ref[:, off:off + width], preferred_element_type=F32)

    for off, ref in ((OFF_DQ, dq_ref), (OFF_DK, dk_ref)):
        if rope:
            cos, sa, sb = cos_ref[...], sa_ref[...], sb_ref[...]
            for hd in range(DA_H):
                u = proj(off + hd * LANES, LANES)
                u = u * cos + pltpu.roll(u, LANES - 16, 1) * sa + pltpu.roll(u, 16, 1) * sb
                ref[:, hd * LANES:(hd + 1) * LANES] = u.astype(ref.dtype)
        else:
            ref[...] = proj(off, DA_W).astype(ref.dtype)
    dv_ref[...] = proj(OFF_DV, DA_W).astype(dv_ref.dtype)
    sz_ref[...] = proj(OFF_SZ, SSM_DI)
    xbc_ref[...] = proj(OFF_XBC, CONV_CH)
    nq_ref[...] = proj(OFF_NQ, NA_W).astype(nq_ref.dtype)
    nk_ref[...] = proj(OFF_NK, NA_W).astype(nk_ref.dtype)
    nv_ref[...] = proj(OFF_NV, NA_W).astype(nv_ref.dtype)
    dt_ref[...] = proj(OFF_DT, DT_PAD)
    gate_ref[...] = _sigmoid(jnp.dot(hb, wg_ref[...], preferred_element_type=F32))


def _pre(x3, mods, g_pre1, w_in_p, w_gate, rope_tabs, qkv_dtype):
    b, n, d = x3.shape
    tm = PRE_TM
    tps = n // tm
    t = b * n
    x2 = x3.reshape(t, d)
    per_batch_mods = mods.shape[0] > 1
    mod_idx = (lambda i: (i // tps, 0, 0)) if per_batch_mods else (lambda i: (0, 0, 0))
    const2 = lambda i: (0, 0)
    row = lambda i: (i, 0)
    in_specs = [pl.BlockSpec((tm, d), row),
                pl.BlockSpec((None, 6, d), mod_idx),
                pl.BlockSpec((1, d), const2),
                pl.BlockSpec((d, IN_WP), const2, pipeline_mode=pl.Buffered(1)),
                pl.BlockSpec((d, 3 * d), const2, pipeline_mode=pl.Buffered(1))]
    args = [x2, mods, g_pre1.reshape(1, d), w_in_p, w_gate]
    rope = rope_tabs is not None
    if rope:
        in_specs += [pl.BlockSpec((tm, LANES), lambda i: (i % tps, 0))] * 3
        args += list(rope_tabs)
    widths = (DA_W, DA_W, DA_W, SSM_DI, CONV_CH, NA_W, NA_W, NA_W, DT_PAD, 3 * d)
    dtypes = (qkv_dtype, qkv_dtype, qkv_dtype, F32, F32, qkv_dtype, qkv_dtype, qkv_dtype, F32, F32)
    outs = pl.pallas_call(
        functools.partial(_pre_kernel, rope=rope),
        grid=(t // tm,),
        in_specs=in_specs,
        out_specs=[pl.BlockSpec((tm, w), row) for w in widths],
        out_shape=[jax.ShapeDtypeStruct((t, w), dt) for w, dt in zip(widths, dtypes)],
        compiler_params=_cparams(("arbitrary",)),
        name="pre_rope" if rope else "pre",
    )(*args)
    return [o.reshape(b, n, o.shape[-1]) for o in outs]


def _da_kernel(*refs, lam_init, has_ctx, tk):
    if has_ctx:
        lam_ref, sub_ref, q_ref, k_ref, v_ref, ck_ref, cv_ref, o_ref, s_ref, p_ref, sc_ref, pc_ref = refs
    else:
        lam_ref, sub_ref, q_ref, k_ref, v_ref, o_ref, s_ref, p_ref = refs
    nsub, _, _, ts, _ = s_ref.shape
    nc = k_ref.shape[0] // tk
    lf = lam_ref[...]
    lam = (jnp.exp(jnp.sum(lf[0:1] * lf[1:2], axis=-1, keepdims=True))
           - jnp.exp(jnp.sum(lf[2:3] * lf[3:4], axis=-1, keepdims=True)) + lam_init)
    lane = lax.broadcasted_iota(jnp.int32, (1, LANES), 1)

    def fold(acc, x, op):
        for j in range(x.shape[1] // LANES):
            acc = op(acc, x[:, j * LANES:(j + 1) * LANES])
        return acc

    def scores(i, t):
        q = q_ref[i * ts:(i + 1) * ts, :].astype(F32) * (DA_HD ** -0.5 * LOG2E)
        qt = jnp.where((lane >= DA_HD) == bool(t), q, 0.0).astype(BF16)
        m_acc = jnp.full((ts, LANES), -jnp.inf, F32)
        for kc in range(nc):
            s = _bdot_nt(qt, k_ref[kc * tk:(kc + 1) * tk, :])
            s_ref[i, t, kc] = s
            m_acc = fold(m_acc, s, jnp.maximum)
        if has_ctx:
            sc = _bdot_nt(qt, ck_ref[...])
            sc_ref[i, t] = sc
            m_acc = fold(m_acc, sc, jnp.maximum)
        return jnp.max(m_acc, axis=-1, keepdims=True)

    def exps(i, t, m):
        l_acc = jnp.zeros((ts, LANES), F32)
        for kc in range(nc):
            p = jnp.exp2(s_ref[i, t, kc] - m)
            p_ref[i, t, kc] = p.astype(BF16)
            l_acc = fold(l_acc, p, jnp.add)
        if has_ctx:
            pc = jnp.exp2(sc_ref[i, t] - m)
            pc_ref[i, t] = pc.astype(BF16)
            l_acc = fold(l_acc, pc, jnp.add)
        return jnp.sum(l_acc, axis=-1, keepdims=True)

    def attend(i, l0, l1):
        r = (lam * l0 / l1).astype(BF16)
        acc = jnp.zeros((ts, LANES), F32)
        for kc in range(nc):
            acc = acc + jnp.dot(p_ref[i, 0, kc] - p_ref[i, 1, kc] * r,
                                v_ref[kc * tk:(kc + 1) * tk, :].astype(BF16), preferred_element_type=F32)
        if has_ctx:
            acc = acc + jnp.dot(pc_ref[i, 0] - pc_ref[i, 1] * r, cv_ref[...].astype(BF16),
                                preferred_element_type=F32)
        o = _rms(acc * (1.0 / l0), sub_ref[...]) * (1.0 - lam_init)
        o_ref[i * ts:(i + 1) * ts, :] = o.astype(o_ref.dtype)

    m, inv = {}, {}
    for i in range(nsub + 1):
        if i < nsub:
            m[i, 0] = scores(i, 0)
        if i > 0:
            inv[i - 1, 1] = exps(i - 1, 1, m[i - 1, 1])
        if i < nsub:
            m[i, 1] = scores(i, 1)
        if i > 0:
            attend(i - 1, inv[i - 1, 0], inv[i - 1, 1])
        if i < nsub:
            inv[i, 0] = exps(i, 0, m[i, 0])


def _da(dq, dk, dv, lam_p, subln, layer, ctx_kv):
    b, n, _ = dq.shape
    lam_init = 0.8 - 0.6 * math.exp(-0.3 * layer)
    tq = min(DA_TQ, n)
    ts = min(DA_TS, tq)
    tk = min(DA_TK, n)
    has_ctx = ctx_kv is not None
    qmap = lambda bi, h, qi: (bi, qi, h)
    kvmap = lambda bi, h, qi: (bi, 0, h)
    in_specs = [pl.BlockSpec((4, DA_HD), lambda bi, h, qi: (0, 0)),
                pl.BlockSpec((1, 2 * DA_HD), lambda bi, h, qi: (0, 0)),
                pl.BlockSpec((None, tq, LANES), qmap),
                pl.BlockSpec((None, n, LANES), kvmap),
                pl.BlockSpec((None, n, LANES), kvmap)]
    args = [lam_p, subln.reshape(1, 2 * DA_HD), dq, dk, dv]
    scratch = [pltpu.VMEM((tq // ts, 2, n // tk, ts, tk), F32), pltpu.VMEM((tq // ts, 2, n // tk, ts, tk), BF16)]
    if has_ctx:
        ck, cv = ctx_kv
        past = ck.shape[3]
        cmap = lambda bi, h, qi: (bi, layer, h, 0, 0)
        in_specs += [pl.BlockSpec((None, None, None, past, LANES), cmap)] * 2
        args += [ck, cv]
        scratch += [pltpu.VMEM((tq // ts, 2, ts, past), F32), pltpu.VMEM((tq // ts, 2, ts, past), BF16)]
    return pl.pallas_call(
        functools.partial(_da_kernel, lam_init=lam_init, has_ctx=has_ctx, tk=tk),
        grid=(b, DA_H, n // tq),
        in_specs=in_specs,
        out_specs=pl.BlockSpec((None, tq, LANES), qmap),
        out_shape=jax.ShapeDtypeStruct((b, n, DA_W), F32),
        scratch_shapes=scratch,
        compiler_params=_cparams(("arbitrary", "arbitrary", "arbitrary")),
        name="da_lat" if has_ctx else "da_ctx",
    )(*args)


def _pair_attend(q, sources):
    lane = lax.broadcasted_iota(jnp.int32, (1, LANES), 1)
    outs = []
    for j in (0, 1):
        qj = jnp.where((lane >= NA_HD) == bool(j), q, 0.0).astype(BF16)
        ss = []
        for k, _, bias in sources:
            s = _bdot_nt(qj, k)
            if bias is not None:
                s = s + bias[j]
            ss.append(s)
        m = ss[0].max(axis=-1, keepdims=True)
        for s in ss[1:]:
            m = jnp.maximum(m, s.max(axis=-1, keepdims=True))
        l = 0.0
        o = 0.0
        for s, (_, v, _) in zip(ss, sources):
            p = jnp.exp2(s - m)
            l = l + p.sum(axis=-1, keepdims=True)
            o = o + _bdot(p, v)
        outs.append(o * (1.0 / l))
    return jnp.where(lane >= NA_HD, outs[1], outs[0])


def _na_lat_kernel(q_ref, k_ref, v_ref, ck_ref, cv_ref, b_ref, o_ref, *, n_rows):
    rb = pl.program_id(2)
    row0 = jnp.clip(rb * WIN_R - WIN_R // 2, 0, n_rows - NA_KROWS)
    start = pl.multiple_of(row0 * GRID_W, GRID_W)
    kw = k_ref[pl.ds(start, NA_KW), :]
    vw = v_ref[pl.ds(start, NA_KW), :]
    q = q_ref[...].astype(F32) * NA_QSCALE
    o_ref[...] = _pair_attend(q, [(kw, vw, b_ref), (ck_ref[...], cv_ref[...], None)])


def _na_ctx_kernel(q_ref, k_ref, v_ref, o_ref):
    q = q_ref[...].astype(F32) * NA_QSCALE
    o_ref[...] = _pair_attend(q, [(k_ref[...], v_ref[...], None)])


def _na_bias_tables(rpb, n_rows):
    nrb = n_rows // WIN_R
    nh = rpb.shape[0]
    pad = GRID_W - WIN_C
    rpad = jnp.pad(rpb.astype(F32), ((0, 0), (0, 0), (pad, pad)))
    toep = jnp.stack([rpad[:, :, GRID_W - 1 - qc:2 * GRID_W - 1 - qc] for qc in range(GRID_W)], axis=2)
    cols = np.arange(GRID_W)
    s_c = np.clip(cols - WIN_C // 2, 0, GRID_W - WIN_C)[:, None]
    ok_c = (cols[None, :] >= s_c) & (cols[None, :] < s_c + WIN_C)
    toep = jnp.where(jnp.asarray(ok_c), toep, -jnp.inf)
    masked = jnp.full((nh, GRID_W, GRID_W), -jnp.inf, F32)
    tabs = []
    for rb in (0, 1, nrb - 1):
        row0 = min(max(rb * WIN_R - WIN_R // 2, 0), n_rows - NA_KROWS)
        q_rows = rb * WIN_R + np.arange(WIN_R)
        k_rows = row0 + np.arange(NA_KROWS)
        s_r = np.clip(q_rows - WIN_R // 2, 0, n_rows - WIN_R)[:, None]
        ok_r = (k_rows[None, :] >= s_r) & (k_rows[None, :] < s_r + WIN_R)
        off_r = np.clip(k_rows[None, :] - q_rows[:, None] + WIN_R - 1, 0, 2 * WIN_R - 2)
        rows = [jnp.concatenate([toep[:, off_r[qr, kr]] if ok_r[qr, kr] else masked for kr in range(NA_KROWS)],
                                axis=-1) for qr in range(WIN_R)]
        tabs.append(jnp.concatenate(rows, axis=-2) * LOG2E)
    return jnp.stack(tabs, axis=1)


def _na_lat(nq, nk, nv, ck, cv, rpb, layer):
    b, n, _ = nq.shape
    n_rows = n // GRID_W
    assert n_rows >= NA_KROWS and n_rows % WIN_R == 0
    nrb = n_rows // WIN_R
    past = ck.shape[3]
    bias = _na_bias_tables(rpb, n_rows)
    kvmap = lambda bi, hp, rb: (bi, 0, hp)
    cmap = lambda bi, hp, rb: (bi, layer, hp, 0, 0)
    variant = lambda bi, hp, rb: (hp, jnp.where(rb == 0, 0, jnp.where(rb == nrb - 1, 2, 1)), 0, 0)
    return pl.pallas_call(
        functools.partial(_na_lat_kernel, n_rows=n_rows),
        grid=(b, NA_H // 2, nrb),
        in_specs=[pl.BlockSpec((None, NA_QB, LANES), lambda bi, hp, rb: (bi, rb, hp)),
                  pl.BlockSpec((None, n, LANES), kvmap),
                  pl.BlockSpec((None, n, LANES), kvmap),
                  pl.BlockSpec((None, None, None, past, LANES), cmap),
                  pl.BlockSpec((None, None, None, past, LANES), cmap),
                  pl.BlockSpec((2, None, NA_QB, NA_KW), variant)],
        out_specs=pl.BlockSpec((None, NA_QB, LANES), lambda bi, hp, rb: (bi, rb, hp)),
        out_shape=jax.ShapeDtypeStruct((b, n, NA_W), F32),
        compiler_params=_cparams(("arbitrary", "arbitrary", "arbitrary")),
        name="na_lat",
    )(nq, nk, nv, ck, cv, bias)


def _na_ctx(nq, nk, nv):
    b, n, _ = nq.shape
    spec = pl.BlockSpec((None, n, LANES), lambda bi, hp: (bi, 0, hp))
    return pl.pallas_call(
        _na_ctx_kernel,
        grid=(b, NA_H // 2),
        in_specs=[spec, spec, spec],
        out_specs=spec,
        out_shape=jax.ShapeDtypeStruct((b, n, NA_W), F32),
        compiler_params=_cparams(("arbitrary", "arbitrary")),
        name="na_ctx",
    )(nq, nk, nv)


def _split3(x):
    hi = x.astype(BF16)
    r = x - hi.astype(F32)
    mid = r.astype(BF16)
    return hi, mid, (r - mid.astype(F32)).astype(BF16)


def _dot3_left(x, m01):
    hi, mid, lo = _split3(x)
    d = lambda a: jnp.dot(a, m01, preferred_element_type=F32)
    return (d(lo) + d(mid)) + d(hi)


def _dot3_right(m01, x):
    hi, mid, lo = _split3(x)
    d = lambda a: jnp.dot(m01, a, preferred_element_type=F32)
    return (d(lo) + d(mid)) + d(hi)


def _ssd_block(refs, sb, *, bwd, ns, cps):
    (xm_ref, xp_ref, xn_ref, z_ref, dt_ref, dtt_ref, cw_ref, cb_ref, dtb_ref, dtbt_ref, alogt_ref,
     alogx_ref, dexp_ref, nrm_ref, y_ref, xpad_ref, st_ref, yacc_ref, act_ref) = refs
    t = SSM_CHUNK
    tb = cps * t
    gw = SSM_DI // SSM_G
    hpg = SSM_H // SSM_G
    p_shift = SSM_P.bit_length() - 1
    row0 = pl.multiple_of(sb * tb, tb)

    if not bwd:
        xpad_ref[0:SUBLANES, :] = jnp.where(sb == 0, 0.0, xp_ref[...])
        xpad_ref[SUBLANES:SUBLANES + tb, :] = xm_ref[...]
        xpad_ref[SUBLANES + tb:, :] = jnp.where(sb == ns - 1, 0.0, xn_ref[...])
        conv = cb_ref[...]
        for k in range(CONV_W):
            o = SUBLANES - CONV_W // 2 + k
            conv = conv + cw_ref[k:k + 1, :] * xpad_ref[o:o + tb, :]
        act = _silu(conv)
        act_ref[pl.ds(row0, tb), :] = act
    else:
        act = act_ref[pl.ds(row0, tb), :]
    xs = act[:, :SSM_DI]
    bm = act[:, SSM_DI:SSM_DI + SSM_G * SSM_N]
    cm = act[:, SSM_DI + SSM_G * SSM_N:]

    li = lax.broadcasted_iota(jnp.int32, (t, t), 0)
    si = lax.broadcasted_iota(jnp.int32, (t, t), 1)
    keep = (si >= li) if bwd else (si <= li)
    tri = jnp.where(keep, 1.0, 0.0).astype(BF16)
    trit = jnp.where((li >= si) if bwd else (li <= si), 1.0, 0.0).astype(BF16)
    r_i = lax.broadcasted_iota(jnp.int32, (LANES, SSM_DI), 0)
    c_i = lax.broadcasted_iota(jnp.int32, (LANES, SSM_DI), 1)
    d = int(bwd)
    expand = jnp.where(r_i == d * SSM_H + (c_i >> p_shift), 1.0, 0.0).astype(BF16)
    dtv = _softplus(dt_ref[...] + dtb_ref[...])
    dtx = _dot3_left(dtv, expand)
    a_x = dtx * (-jnp.exp(alogx_ref[d:d + 1, :]))
    xdt = xs * dtx
    at = -jnp.exp(alogt_ref[...]) * _softplus(dtt_ref[...] + dtbt_ref[...])
    lane_g = lax.broadcasted_iota(jnp.int32, (1, gw), 1)

    order = range(cps - 1, -1, -1) if bwd else range(cps)
    y_rows = [None] * cps
    for ci in order:
        rows = slice(ci * t, (ci + 1) * t)
        acum = _dot3_right(tri, a_x[rows])
        tot = acum[0:1, :] if bwd else acum[t - 1:t, :]
        e_in = jnp.exp(acum)
        xdt_c = xdt[rows]
        xdt_b = xdt_c.astype(BF16)
        xout_b = (xdt_c * jnp.exp(tot - acum)).astype(BF16)
        acum_t = _dot3_left(at[:, rows], trit)
        y_parts = []
        for g in range(SSM_G):
            cg = cm[rows, g * SSM_N:(g + 1) * SSM_N].astype(BF16)
            bg = bm[rows, g * SSM_N:(g + 1) * SSM_N]
            gmat = _bdot_nt(cg, bg)
            st_g = st_ref[:, g * gw:(g + 1) * gw]
            y_g = _bdot(cg, st_g) * e_in[:, g * gw:(g + 1) * gw]
            for hh in range(hpg):
                h = g * hpg + hh
                col = acum[:, h * SSM_P:h * SSM_P + 1]
                row = acum_t[d * SSM_H + h:d * SSM_H + h + 1, :]
                lmat = jnp.exp(jnp.where(keep, col - row, -jnp.inf))
                yh = _bdot(gmat * lmat, xdt_b[:, g * gw:(g + 1) * gw])
                y_g = y_g + jnp.where((lane_g >> p_shift) == hh, yh, 0.0)
            y_parts.append(y_g)
            st_ref[:, g * gw:(g + 1) * gw] = (st_g * jnp.exp(tot[:, g * gw:(g + 1) * gw])
                                              + _bdot(bg.T, xout_b[:, g * gw:(g + 1) * gw]))
        y_rows[ci] = jnp.concatenate(y_parts, axis=-1)
    y_dir = jnp.concatenate(y_rows, axis=0)

    if not bwd:
        yacc_ref[pl.ds(row0, tb), :] = y_dir + xs * dexp_ref[...]
    else:
        y = (yacc_ref[pl.ds(row0, tb), :] + y_dir) * _silu(z_ref[...])
        parts = [_rms(y[:, g * gw:(g + 1) * gw], nrm_ref[:, g * gw:(g + 1) * gw]) for g in range(SSM_G)]
        y_ref[...] = jnp.concatenate(parts, axis=-1)


def _ssd_kernel(*refs, ns, cps, has_h0):
    n_in = 14
    if has_h0:
        h0f_ref, h0b_ref = refs[n_in:n_in + 2]
        refs = refs[:n_in] + refs[n_in + 2:]
    hf_ref, hb_ref = refs[n_in + 1:n_in + 3]
    block_refs = refs[:n_in + 1] + refs[n_in + 3:]
    st_ref = block_refs[n_in + 2]
    j = pl.program_id(1)

    @pl.when(j == 0)
    def _():
        st_ref[...] = h0f_ref[...] if has_h0 else jnp.zeros_like(st_ref)

    @pl.when(j == ns)
    def _():
        st_ref[...] = h0b_ref[...] if has_h0 else jnp.zeros_like(st_ref)

    @pl.when(j < ns)
    def _():
        _ssd_block(block_refs, j, bwd=False, ns=ns, cps=cps)

    @pl.when(j >= ns)
    def _():
        _ssd_block(block_refs, 2 * ns - 1 - j, bwd=True, ns=ns, cps=cps)

    @pl.when(j == ns - 1)
    def _():
        hf_ref[...] = st_ref[...]

    @pl.when(j == 2 * ns - 1)
    def _():
        hb_ref[...] = st_ref[...]


def _ssd(sz, sxbc, sdt, p, h0):
    b, n, _ = sz.shape
    t = SSM_CHUNK
    cps = min(SSD_CPS, n // t)
    tb = cps * t
    assert n % tb == 0
    ns = n // tb
    nb8 = n // SUBLANES
    tb8 = tb // SUBLANES
    has_h0 = h0 is not None
    sdt_t = jnp.swapaxes(sdt[:, :, :2 * SSM_H], 1, 2)
    sidx = lambda j: jnp.where(j >= ns, 2 * ns - 1 - j, j)
    fidx = lambda j: jnp.minimum(j, ns - 1)
    zidx = lambda j: jnp.where(j >= ns, 2 * ns - 1 - j, ns - 1)
    const2 = lambda bi, j: (0, 0)
    cw = jnp.zeros((SUBLANES, CONV_CH), F32).at[:CONV_W].set(p["ssm_conv_w"])
    dtb = jnp.zeros((1, DT_PAD), F32).at[0, :2 * SSM_H].set(p["ssm_dt_bias"].reshape(-1))
    dtbt = jnp.broadcast_to(p["ssm_dt_bias"].reshape(2 * SSM_H, 1), (2 * SSM_H, tb))
    alogt = jnp.broadcast_to(p["ssm_a_log"].reshape(2 * SSM_H, 1), (2 * SSM_H, tb))
    alogx = jnp.repeat(p["ssm_a_log"], SSM_P, axis=-1)
    dexp = jnp.repeat(p["ssm_d"], SSM_P).reshape(1, SSM_DI)
    in_specs = [pl.BlockSpec((None, tb, CONV_CH), lambda bi, j: (bi, fidx(j), 0)),
                pl.BlockSpec((None, SUBLANES, CONV_CH), lambda bi, j: (bi, jnp.maximum(fidx(j) * tb8 - 1, 0), 0)),
                pl.BlockSpec((None, SUBLANES, CONV_CH),
                             lambda bi, j: (bi, jnp.minimum((fidx(j) + 1) * tb8, nb8 - 1), 0)),
                pl.BlockSpec((None, tb, SSM_DI), lambda bi, j: (bi, zidx(j), 0)),
                pl.BlockSpec((None, tb, DT_PAD), lambda bi, j: (bi, sidx(j), 0)),
                pl.BlockSpec((None, 2 * SSM_H, tb), lambda bi, j: (bi, 0, sidx(j))),
                pl.BlockSpec((SUBLANES, CONV_CH), const2),
                pl.BlockSpec((1, CONV_CH), const2),
                pl.BlockSpec((1, DT_PAD), const2),
                pl.BlockSpec((2 * SSM_H, tb), const2),
                pl.BlockSpec((2 * SSM_H, tb), const2),
                pl.BlockSpec((2, SSM_DI), const2),
                pl.BlockSpec((1, SSM_DI), const2),
                pl.BlockSpec((1, SSM_DI), const2)]
    args = [sxbc, sxbc, sxbc, sz, sdt, sdt_t, cw, p["ssm_conv_b"].reshape(1, CONV_CH), dtb, dtbt,
            alogt, alogx, dexp, p["ssm_norm"].reshape(1, SSM_DI)]
    st_spec = pl.BlockSpec((None, SSM_N, SSM_DI), lambda bi, j: (bi, 0, 0))
    if has_h0:
        in_specs += [st_spec, st_spec]
        args += list(h0)
    y, hf, hb = pl.pallas_call(
        functools.partial(_ssd_kernel, ns=ns, cps=cps, has_h0=has_h0),
        grid=(b, 2 * ns),
        in_specs=in_specs,
        out_specs=[pl.BlockSpec((None, tb, SSM_DI), lambda bi, j: (bi, zidx(j), 0)), st_spec, st_spec],
        out_shape=[jax.ShapeDtypeStruct((b, n, SSM_DI), F32),
                   jax.ShapeDtypeStruct((b, SSM_N, SSM_DI), F32),
                   jax.ShapeDtypeStruct((b, SSM_N, SSM_DI), F32)],
        scratch_shapes=[pltpu.VMEM((tb + 2 * SUBLANES, CONV_CH), F32),
                        pltpu.VMEM((SSM_N, SSM_DI), F32),
                        pltpu.VMEM((n, SSM_DI), F32),
                        pltpu.VMEM((n, CONV_CH), F32)],
        compiler_params=_cparams(("arbitrary", "arbitrary")),
        name="ssd_lat" if has_h0 else "ssd_ctx",
    )(*args)
    return y, hf, hb


def _post_kernel(x_ref, oda_ref, ossm_ref, ona_ref, gate_ref, mod_ref, wda_ref, wssm_ref, wna_ref, wout_ref,
                 gpost_ref, gpre_ref, wr_ref, br_ref, x1_ref, h2_ref, rg_ref):
    d = x_ref.shape[1]
    tm = x_ref.shape[0]
    ts = min(POST_TS, tm)
    for i in range(tm // ts):
        r = slice(i * ts, (i + 1) * ts)
        mix = (gate_ref[r, 0:d] * _bdot(oda_ref[r, :], wda_ref[...])
               + gate_ref[r, d:2 * d] * _bdot(ossm_ref[r, :], wssm_ref[...])
               + gate_ref[r, 2 * d:3 * d] * _bdot(ona_ref[r, :], wna_ref[...]))
        mix = _bdot(mix, wout_ref[...])
        x1 = x_ref[r, :] + mod_ref[2:3, :] * _rms(mix, gpost_ref[...])
        x1_ref[r, :] = x1
        h2 = _rms(x1, gpre_ref[...]) * (1.0 + mod_ref[4:5, :]) + mod_ref[3:4, :]
        h_hi = h2.astype(BF16)
        h2_ref[r, :] = h_hi

        h_mid = (h2 - h_hi.astype(F32)).astype(BF16)
        a = jnp.dot(h_hi, wr_ref[...], preferred_element_type=F32)
        b = jnp.dot(h_mid, wr_ref[:, :LANES], preferred_element_type=F32)
        s = _sigmoid(a[:, :LANES] + (a[:, LANES:] + b))
        work = s + br_ref[...]
        col = lax.broadcasted_iota(jnp.int32, (ts, LANES), 1).astype(F32)
        picked = jnp.zeros((ts, LANES), F32)
        for _ in range(TOP_K):
            m = jnp.max(work, axis=-1, keepdims=True)
            first = jnp.min(jnp.where(work == m, col, float(LANES)), axis=-1, keepdims=True)
            hit = col == first
            picked = jnp.where(hit, 1.0, picked)
            work = jnp.where(hit, -jnp.inf, work)
        sel = picked * s
        g = sel / jnp.sum(sel, axis=-1, keepdims=True) * ROUTE_SCALE
        g_hi = g.astype(BF16).astype(F32)
        r1 = g - g_hi
        g_mid = r1.astype(BF16).astype(F32)
        g_lo = r1 - g_mid
        packed = g_hi + pltpu.roll(g_mid, N_EXPERTS, 1) + pltpu.roll(g_lo, 2 * N_EXPERTS, 1)
        rg_ref[r, :] = packed.astype(rg_ref.dtype)


def _post(x3, o_da, o_ssm, o_na, gates, mods, p):
    b, n, d = x3.shape
    tm = min(POST_TM, n)
    tps = n // tm
    t = b * n
    per_batch_mods = mods.shape[0] > 1
    mod_idx = (lambda i: (i // tps, 0, 0)) if per_batch_mods else (lambda i: (0, 0, 0))
    row = lambda i: (i, 0)
    const2 = lambda i: (0, 0)
    flat = lambda a: a.reshape(t, a.shape[-1])
    wspec = lambda k: pl.BlockSpec((k, d), const2)
    w_r = jnp.zeros((d, LANES), F32).at[:, :N_EXPERTS].set(p["w_router"])
    w_r_hi = w_r.astype(BF16)
    w_router = jnp.concatenate([w_r_hi, (w_r - w_r_hi.astype(F32)).astype(BF16)], axis=1)
    b_router = jnp.full((1, LANES), -jnp.inf, F32).at[0, :N_EXPERTS].set(p["b_router"])
    x1, h2, route = pl.pallas_call(
        _post_kernel,
        grid=(t // tm,),
        in_specs=[pl.BlockSpec((tm, d), row), pl.BlockSpec((tm, DA_W), row), pl.BlockSpec((tm, SSM_DI), row),
                  pl.BlockSpec((tm, NA_W), row), pl.BlockSpec((tm, 3 * d), row),
                  pl.BlockSpec((None, 6, d), mod_idx),
                  wspec(DA_W), wspec(SSM_DI), wspec(NA_W), wspec(d),
                  pl.BlockSpec((1, d), const2), pl.BlockSpec((1, d), const2),
                  pl.BlockSpec((d, 2 * LANES), const2), pl.BlockSpec((1, LANES), const2)],
        out_specs=[pl.BlockSpec((tm, d), row), pl.BlockSpec((tm, d), row), pl.BlockSpec((tm, LANES), row)],
        out_shape=[jax.ShapeDtypeStruct((t, d), F32), jax.ShapeDtypeStruct((t, d), BF16),
                   jax.ShapeDtypeStruct((t, LANES), BF16)],
        compiler_params=_cparams(("arbitrary",)),
        name="post",
    )(flat(x3), flat(o_da), flat(o_ssm), flat(o_na), flat(gates), mods,
      p["w_br_da"], p["w_br_ssm"], p["w_br_na"], p["w_out"],
      p["g_post1"].reshape(1, d), p["g_pre2"].reshape(1, d), w_router, b_router)
    return x1, h2, route


def _moe_kernel(x1_ref, h2_ref, rg_ref, mod_ref, wsg_ref, wsu_ref, wsd_ref, weg_ref, weu_ref, wed_ref,
                gpost_ref, o_ref, acc_ref):
    e = pl.program_id(1)
    hb = h2_ref[...]

    @pl.when(e == 0)
    def _():
        acc_ref[...] = _bdot(_silu(_bdot(hb, wsg_ref[...])) * _bdot(hb, wsu_ref[...]), wsd_ref[...])

    width = MOE_EG * EXPERT_FF
    r_i = lax.broadcasted_iota(jnp.int32, (LANES, width), 0)
    c_i = lax.broadcasted_iota(jnp.int32, (LANES, width), 1)
    ff_shift = EXPERT_FF.bit_length() - 1
    onehot = jnp.where((r_i < 3 * N_EXPERTS) & ((r_i & (N_EXPERTS - 1)) == e * MOE_EG + (c_i >> ff_shift)),
                       1.0, 0.0).astype(BF16)
    g_exp = jnp.dot(rg_ref[...], onehot, preferred_element_type=F32)
    act = _silu(_bdot(hb, weg_ref[...])) * _bdot(hb, weu_ref[...]) * g_exp
    acc_ref[...] += _bdot(act, wed_ref[...])

    @pl.when(e == pl.num_programs(1) - 1)
    def _():
        o_ref[...] = x1_ref[...] + mod_ref[5:6, :] * _rms(acc_ref[...], gpost_ref[...])


def _moe(x1, h2, route, mods, p, n):
    t, d = x1.shape
    tm = MOE_TM
    per_batch_mods = mods.shape[0] > 1
    if per_batch_mods:
        assert n % tm == 0
    tps = max(n // tm, 1)
    mod_idx = (lambda i, e: (i // tps, 0, 0)) if per_batch_mods else (lambda i, e: (0, 0, 0))
    row = lambda i, e: (i, 0)
    const2 = lambda i, e: (0, 0)
    width = MOE_EG * EXPERT_FF
    return pl.pallas_call(
        _moe_kernel,
        grid=(t // tm, N_EXPERTS // MOE_EG),
        in_specs=[pl.BlockSpec((tm, d), row), pl.BlockSpec((tm, d), row), pl.BlockSpec((tm, LANES), row),
                  pl.BlockSpec((None, 6, d), mod_idx),
                  pl.BlockSpec((d, SHARED_FF), const2), pl.BlockSpec((d, SHARED_FF), const2),
                  pl.BlockSpec((SHARED_FF, d), const2),
                  pl.BlockSpec((d, width), lambda i, e: (0, e)),
                  pl.BlockSpec((d, width), lambda i, e: (0, e)),
                  pl.BlockSpec((width, d), lambda i, e: (e, 0)),
                  pl.BlockSpec((1, d), const2)],
        out_specs=pl.BlockSpec((tm, d), row),
        out_shape=jax.ShapeDtypeStruct((t, d), F32),
        scratch_shapes=[pltpu.VMEM((tm, d), F32)],
        compiler_params=_cparams(("arbitrary", "arbitrary")),
        name="moe",
    )(x1, h2, route, mods,
      p["w_s_gate"], p["w_s_up"], p["w_s_down"], p["w_e_gate"], p["w_e_up"], p["w_e_down"],
      p["g_post2"].reshape(1, d))


def _rope_tables(n):
    pos = jnp.arange(n)
    rows = (pos // GRID_W).astype(F32)
    cols = (pos % GRID_W).astype(F32)
    n_freq = DA_HD // 4
    inv = ROPE_BASE ** (-jnp.arange(n_freq, dtype=F32) / n_freq)
    ar, ac = rows[:, None] * inv, cols[:, None] * inv
    zero = jnp.zeros_like(ar)

    def lanes(first_r, second_r, first_c, second_c):
        half = jnp.concatenate([first_r, second_r, first_c, second_c], axis=-1)
        return jnp.concatenate([half, half], axis=-1)

    cos = lanes(jnp.cos(ar), jnp.cos(ar), jnp.cos(ac), jnp.cos(ac))
    s_first = lanes(-jnp.sin(ar), zero, -jnp.sin(ac), zero)
    s_second = lanes(zero, jnp.sin(ar), zero, jnp.sin(ac))
    return cos, s_first, s_second


def _prep_layer_params(stacked, layer):
    p = {name: arr[layer] for name, arr in stacked.items()}
    w_in = p["w_in"]
    dt0 = 3 * DA_W + SSM_DI + CONV_CH
    d = w_in.shape[0]
    p["w_in_p"] = jnp.concatenate(
        [w_in[:, :dt0], w_in[:, dt0 + 2 * SSM_H:], w_in[:, dt0:dt0 + 2 * SSM_H],
         jnp.zeros((d, DT_PAD - 2 * SSM_H), w_in.dtype)], axis=-1).astype(BF16)
    for name in ("w_gate", "w_br_da", "w_br_ssm", "w_br_na", "w_out", "w_s_gate", "w_s_up", "w_s_down"):
        p[name] = p[name].astype(BF16)
    ff = N_EXPERTS * EXPERT_FF
    p["w_e_gate"] = p["w_e_gate"].astype(BF16).transpose(1, 0, 2).reshape(d, ff)
    p["w_e_up"] = p["w_e_up"].astype(BF16).transpose(1, 0, 2).reshape(d, ff)
    p["w_e_down"] = p["w_e_down"].astype(BF16).reshape(ff, d)
    return p


def _trunk_layer(x3, mods, p, layer, ctx):
    b, n, d = x3.shape
    latent = ctx is not None
    rope_tabs = _rope_tables(n) if latent else None
    qkv_dtype = BF16 if latent else F32
    dq, dk, dv, sz, sxbc, nq, nk, nv, sdt, gates = _pre(x3, mods, p["g_pre1"], p["w_in_p"], p["w_gate"],
                                                        rope_tabs, qkv_dtype)
    if latent:
        ck, cv, cnk, cnv, h0f, h0b = ctx
        o_da = _da(dq, dk, dv, p["da_lambda"], p["da_subln"], layer, (ck, cv))
        o_na = _na_lat(nq, nk, nv, cnk, cnv, p["na_rpb"], layer)
        o_ssm, hf, hb = _ssd(sz, sxbc, sdt, p, (h0f, h0b))
    else:
        o_da = _da(dq, dk, dv, p["da_lambda"], p["da_subln"], layer, None)
        o_na = _na_ctx(nq, nk, nv)
        o_ssm, hf, hb = _ssd(sz, sxbc, sdt, p, None)
    x1, h2, route = _post(x3, o_da, o_ssm, o_na, gates, mods, p)
    x2 = _moe(x1, h2, route, mods, p, n).reshape(b, n, d)
    state = None
    if not latent:
        to_state = lambda s: s.reshape(b, SSM_N, SSM_H, SSM_P).transpose(0, 2, 3, 1)
        state = (dk.reshape(b, n, DA_H, 2, DA_HD).transpose(0, 2, 1, 3, 4),
                 dv.reshape(b, n, DA_H, 2 * DA_HD).transpose(0, 2, 1, 3),
                 nk.reshape(b, n, NA_H, NA_HD).transpose(0, 2, 1, 3),
                 nv.reshape(b, n, NA_H, NA_HD).transpose(0, 2, 1, 3),
                 to_state(hf), to_state(hb))
    return x2, state


def kernel(x_prompt, x_sample, c, cache_da_k, cache_da_v, cache_na_k, cache_na_v, state_ssm_f, state_ssm_b,
           c_ctx, w_ada, b_ada, g_pre1, g_post1, g_pre2, g_post2, w_in, w_gate, da_lambda, da_subln,
           ssm_conv_w, ssm_conv_b, ssm_dt_bias, ssm_a_log, ssm_d, ssm_norm, na_rpb, w_br_da, w_br_ssm,
           w_br_na, w_out, w_router, b_router, w_e_gate, w_e_up, w_e_down, w_s_gate, w_s_up, w_s_down):
    stacked = {
        "g_pre1": g_pre1, "g_post1": g_post1, "g_pre2": g_pre2, "g_post2": g_post2, "w_in": w_in,
        "w_gate": w_gate, "da_lambda": da_lambda, "da_subln": da_subln, "ssm_conv_w": ssm_conv_w,
        "ssm_conv_b": ssm_conv_b, "ssm_dt_bias": ssm_dt_bias, "ssm_a_log": ssm_a_log, "ssm_d": ssm_d,
        "ssm_norm": ssm_norm, "na_rpb": na_rpb, "w_br_da": w_br_da, "w_br_ssm": w_br_ssm, "w_br_na": w_br_na,
        "w_out": w_out, "w_router": w_router, "b_router": b_router, "w_e_gate": w_e_gate, "w_e_up": w_e_up,
        "w_e_down": w_e_down, "w_s_gate": w_s_gate, "w_s_up": w_s_up, "w_s_down": w_s_down,
    }
    depth = w_ada.shape[0]
    nb = x_sample.shape[0]
    d = x_prompt.shape[-1]
    rows = -(-(1 + nb) // SUBLANES) * SUBLANES
    cvec = jnp.zeros((rows, d), F32).at[0].set(c_ctx).at[1:1 + nb].set(c)
    mods = _ada(cvec, w_ada, b_ada).reshape(depth, rows, 6, d)

    past = cache_da_k.shape[3]
    ck_all = cache_da_k.reshape(nb, depth, DA_H, past, 2 * DA_HD)
    pair = lambda a: a.reshape(nb, depth, NA_H // 2, 2, past, NA_HD).transpose(0, 1, 2, 4, 3, 5).reshape(
        nb, depth, NA_H // 2, past, 2 * NA_HD)
    cnk_all, cnv_all = pair(cache_na_k), pair(cache_na_v)
    st_t = lambda s: s.transpose(0, 1, 4, 2, 3).reshape(nb, depth, SSM_N, SSM_DI)
    h0f_all, h0b_all = st_t(state_ssm_f), st_t(state_ssm_b)

    xp, xs = x_prompt, x_sample
    new = ([], [], [], [], [], [])
    for layer in range(depth):
        p = _prep_layer_params(stacked, layer)
        xp, st = _trunk_layer(xp, mods[layer, 0:1], p, layer, None)
        for lst, arr in zip(new, st):
            lst.append(arr)
        ctx = (ck_all, cache_da_v, cnk_all, cnv_all, h0f_all[:, layer], h0b_all[:, layer])
        xs, _ = _trunk_layer(xs, mods[layer, 1:1 + nb], p, layer, ctx)
    return (xp, xs) + tuple(jnp.stack(lst, axis=1) for lst in new)
```

```python
import functools
import math

import numpy as np
import jax
import jax.numpy as jnp
from jax import lax
from jax.experimental import pallas as pl
from jax.experimental.pallas import tpu as pltpu

F32 = jnp.float32
BF16 = jnp.bfloat16

D_MODEL = 1024
GRID_W = 64
EPS = 1e-6
DA_H, DA_HD = 4, 64
DA_W = DA_H * 2 * DA_HD
ROPE_BASE = 10000.0
SSM_H, SSM_P, SSM_G, SSM_N = 8, 64, 2, 128
SSM_DI = SSM_H * SSM_P
SSM_CHUNK = 128
CONV_W = 5
CONV_CH = SSM_DI + 2 * SSM_G * SSM_N
NA_H, NA_HD = 8, 64
NA_W = NA_H * NA_HD
WIN_R, WIN_C = 8, 16
N_EXPERTS, TOP_K, EXPERT_FF, SHARED_FF = 32, 4, 128, 256
ROUTE_SCALE = 2.5

LANES = 128
SUBLANES = 8
VMEM_LIMIT_BYTES = 56 * 2**20

PRE_TM = 256
POST_TM = 512
POST_TS = 512
MOE_TM = 256
MOE_EG = 32
DA_TQ = 512
DA_TS = 256
SSD_CPS = 4
DA_BOUND_SLACK = 1.0 + 2.0 ** -6
DA_MIN_L = 2.0 ** -60
DA_TK = 512
NA_QROWS = WIN_R // 2
NA_QB = NA_QROWS * GRID_W
NA_SUBS = 4
NA_KROWS = NA_QROWS + WIN_R
NA_KW = NA_KROWS * GRID_W
DT_PAD = LANES

OFF_DQ, OFF_DK, OFF_DV, OFF_SZ, OFF_XBC, OFF_NQ, OFF_NK, OFF_NV, OFF_DT = (
    0, 512, 1024, 1536, 2048, 3072, 3584, 4096, 4608)
IN_WP = OFF_DT + DT_PAD

_NT = (((1,), (1,)), ((), ()))
LOG2E = math.log2(math.e)
NA_QSCALE = NA_HD ** -0.5 * LOG2E


def _cparams(sem):
    return pltpu.CompilerParams(dimension_semantics=sem, vmem_limit_bytes=VMEM_LIMIT_BYTES)


def _bdot(a, b):
    return jnp.dot(a.astype(BF16), b.astype(BF16), preferred_element_type=F32)


def _bdot_nt(a, b):
    return lax.dot_general(a.astype(BF16), b.astype(BF16), _NT, preferred_element_type=F32)


def _fdot(a, b):
    return jnp.dot(a, b, preferred_element_type=F32, precision=lax.Precision.HIGHEST)


def _sigmoid(x):
    return 1.0 / (1.0 + jnp.exp(-x))


def _silu(x):
    return x * _sigmoid(x)


def _softplus(x):
    return jnp.maximum(x, 0.0) + jnp.log(1.0 + jnp.exp(-jnp.abs(x)))


def _rms(x, g):
    return x * lax.rsqrt(jnp.mean(x * x, axis=-1, keepdims=True) + EPS) * g


def _ada_kernel(c_ref, w_ref, b_ref, o_ref):
    o_ref[...] = _bdot(_silu(c_ref[...]), w_ref[...]) + b_ref[...]


def _ada(cvec, w_ada, b_ada):
    nl, d, d6 = w_ada.shape
    r = cvec.shape[0]
    tn = 1024
    return pl.pallas_call(
        _ada_kernel,
        grid=(nl, d6 // tn),
        in_specs=[pl.BlockSpec((r, d), lambda l, j: (0, 0)),
                  pl.BlockSpec((None, d, tn), lambda l, j: (l, 0, j)),
                  pl.BlockSpec((None, 1, tn), lambda l, j: (l, 0, j))],
        out_specs=pl.BlockSpec((None, r, tn), lambda l, j: (l, 0, j)),
        out_shape=jax.ShapeDtypeStruct((nl, r, d6), F32),
        compiler_params=_cparams(("arbitrary", "arbitrary")),
        name="ada",
    )(cvec, w_ada, b_ada.reshape(nl, 1, d6))


def _pre_kernel(*refs, rope):
    if rope:
        x_ref, mod_ref, g_ref, win_ref, wg_ref, cos_ref, sa_ref, sb_ref = refs[:8]
        outs = refs[8:]
    else:
        x_ref, mod_ref, g_ref, win_ref, wg_ref = refs[:5]
        outs = refs[5:]
    dq_ref, dk_ref, dv_ref, sz_ref, xbc_ref, nq_ref, nk_ref, nv_ref, dt_ref, gate_ref = outs[:10]
    x = x_ref[...]
    h = _rms(x, g_ref[...]) * (1.0 + mod_ref[1:2, :]) + mod_ref[0:1, :]
    hb = h.astype(BF16)

    def proj(off, width):
        return jnp.dot(hb, win_ref[:, off:off + width], preferred_element_type=F32)

    lane = lax.broadcasted_iota(jnp.int32, (1, LANES), 1)
    for off, ref in ((OFF_DQ, dq_ref), (OFF_DK, dk_ref)):
        if rope:
            cos, sa, sb = cos_ref[...], sa_ref[...], sb_ref[...]
            kn = jnp.zeros((x.shape[0], LANES), F32)
            for hd in range(DA_H):
                u = proj(off + hd * LANES, LANES)
                u = u * cos + pltpu.roll(u, LANES - 16, 1) * sa + pltpu.roll(u, 16, 1) * sb
                ub = u.astype(ref.dtype)
                ref[:, hd * LANES:(hd + 1) * LANES] = ub
                if ref is dk_ref:
                    sq = ub.astype(F32) * ub.astype(F32)
                    for t in (0, 1):
                        nt = jnp.sum(jnp.where((lane >= DA_HD) == bool(t), sq, 0.0), axis=-1, keepdims=True)
                        kn = jnp.where(lane == 2 * hd + t, nt, kn)
            if ref is dk_ref:
                outs[10][...] = kn
        else:
            ref[...] = proj(off, DA_W).astype(ref.dtype)
    dv_ref[...] = proj(OFF_DV, DA_W).astype(dv_ref.dtype)
    sz_ref[...] = proj(OFF_SZ, SSM_DI)
    xbc_ref[...] = proj(OFF_XBC, CONV_CH)
    nq_ref[...] = proj(OFF_NQ, NA_W).astype(nq_ref.dtype)
    nk_ref[...] = proj(OFF_NK, NA_W).astype(nk_ref.dtype)
    nv_ref[...] = proj(OFF_NV, NA_W).astype(nv_ref.dtype)
    dt_ref[...] = proj(OFF_DT, DT_PAD)
    gate_ref[...] = _sigmoid(jnp.dot(hb, wg_ref[...], preferred_element_type=F32))


def _pre(x3, mods, g_pre1, w_in_p, w_gate, rope_tabs, qkv_dtype):
    b, n, d = x3.shape
    tm = PRE_TM
    tps = n // tm
    t = b * n
    x2 = x3.reshape(t, d)
    per_batch_mods = mods.shape[0] > 1
    mod_idx = (lambda i: (i // tps, 0, 0)) if per_batch_mods else (lambda i: (0, 0, 0))
    const2 = lambda i: (0, 0)
    row = lambda i: (i, 0)
    in_specs = [pl.BlockSpec((tm, d), row),
                pl.BlockSpec((None, 6, d), mod_idx),
                pl.BlockSpec((1, d), const2),
                pl.BlockSpec((d, IN_WP), const2, pipeline_mode=pl.Buffered(1)),
                pl.BlockSpec((d, 3 * d), const2, pipeline_mode=pl.Buffered(1))]
    args = [x2, mods, g_pre1.reshape(1, d), w_in_p, w_gate]
    rope = rope_tabs is not None
    if rope:
        in_specs += [pl.BlockSpec((tm, LANES), lambda i: (i % tps, 0))] * 3
        args += list(rope_tabs)
    widths = (DA_W, DA_W, DA_W, SSM_DI, CONV_CH, NA_W, NA_W, NA_W, DT_PAD, 3 * d)
    dtypes = (qkv_dtype, qkv_dtype, qkv_dtype, F32, F32, qkv_dtype, qkv_dtype, qkv_dtype, F32, F32)
    if rope:
        widths += (LANES,)
        dtypes += (F32,)
    outs = pl.pallas_call(
        functools.partial(_pre_kernel, rope=rope),
        grid=(t // tm,),
        in_specs=in_specs,
        out_specs=[pl.BlockSpec((tm, w), row) for w in widths],
        out_shape=[jax.ShapeDtypeStruct((t, w), dt) for w, dt in zip(widths, dtypes)],
        compiler_params=_cparams(("arbitrary",)),
        name="pre_rope" if rope else "pre",
    )(*args)
    return [o.reshape(b, n, o.shape[-1]) for o in outs]


def _da_kernel(*refs, lam_init, has_ctx, tk):
    if has_ctx:
        (lam_ref, sub_ref, q_ref, k_ref, v_ref, ck_ref, cv_ref, kn_ref, o_ref, s_ref, p_ref, sc_ref, pc_ref,
         ku_ref) = refs
    else:
        lam_ref, sub_ref, q_ref, k_ref, v_ref, o_ref, s_ref, p_ref = refs
    nsub, _, _, ts, _ = s_ref.shape
    nc = k_ref.shape[0] // tk
    lf = lam_ref[...]
    lam = (jnp.exp(jnp.sum(lf[0:1] * lf[1:2], axis=-1, keepdims=True))
           - jnp.exp(jnp.sum(lf[2:3] * lf[3:4], axis=-1, keepdims=True)) + lam_init)
    lane = lax.broadcasted_iota(jnp.int32, (1, LANES), 1)

    def fold(acc, x, op):
        for j in range(x.shape[1] // LANES):
            acc = op(acc, x[:, j * LANES:(j + 1) * LANES])
        return acc

    def scores(i, t):
        q = q_ref[i * ts:(i + 1) * ts, :].astype(F32) * (DA_HD ** -0.5 * LOG2E)
        qt = jnp.where((lane >= DA_HD) == bool(t), q, 0.0).astype(BF16)
        m_acc = jnp.full((ts, LANES), -jnp.inf, F32)
        for kc in range(nc):
            s = _bdot_nt(qt, k_ref[kc * tk:(kc + 1) * tk, :])
            s_ref[i, t, kc] = s
            m_acc = fold(m_acc, s, jnp.maximum)
        if has_ctx:
            sc = _bdot_nt(qt, ck_ref[...])
            sc_ref[i, t] = sc
            m_acc = fold(m_acc, sc, jnp.maximum)
        return jnp.max(m_acc, axis=-1, keepdims=True)

    def exps(i, t, m):
        l_acc = jnp.zeros((ts, LANES), F32)
        for kc in range(nc):
            p = jnp.exp2(s_ref[i, t, kc] - m)
            p_ref[i, t, kc] = p.astype(BF16)
            l_acc = fold(l_acc, p, jnp.add)
        if has_ctx:
            pc = jnp.exp2(sc_ref[i, t] - m)
            pc_ref[i, t] = pc.astype(BF16)
            l_acc = fold(l_acc, pc, jnp.add)
        return jnp.sum(l_acc, axis=-1, keepdims=True)

    def attend(i, l0, l1):
        r = (lam * l0 / l1).astype(BF16)
        acc = jnp.zeros((ts, LANES), F32)
        for kc in range(nc):
            acc = acc + jnp.dot(p_ref[i, 0, kc] - p_ref[i, 1, kc] * r,
                                v_ref[kc * tk:(kc + 1) * tk, :].astype(BF16), preferred_element_type=F32)
        if has_ctx:
            acc = acc + jnp.dot(pc_ref[i, 0] - pc_ref[i, 1] * r, cv_ref[...].astype(BF16),
                                preferred_element_type=F32)
        o = _rms(acc * (1.0 / l0), sub_ref[...]) * (1.0 - lam_init)
        o_ref[i * ts:(i + 1) * ts, :] = o.astype(o_ref.dtype)

    def exact_step():
        m, inv = {}, {}
        for i in range(nsub + 1):
            if i < nsub:
                m[i, 0] = scores(i, 0)
            if i > 0:
                inv[i - 1, 1] = exps(i - 1, 1, m[i - 1, 1])
            if i < nsub:
                m[i, 1] = scores(i, 1)
            if i > 0:
                attend(i - 1, inv[i - 1, 0], inv[i - 1, 1])
            if i < nsub:
                inv[i, 0] = exps(i, 0, m[i, 0])

    if not has_ctx:
        exact_step()
        return

    @pl.when(pl.program_id(2) == 0)
    def _():
        hd = pl.program_id(1)
        kmax = jnp.max(kn_ref[...], axis=0, keepdims=True)
        ckf = ck_ref[...].astype(BF16).astype(F32)
        c2 = ckf * ckf
        for t in (0, 1):
            k_lat = jnp.max(jnp.where(lane == 2 * hd + t, kmax, 0.0), axis=-1, keepdims=True)
            k_ctx = jnp.max(jnp.sum(jnp.where((lane >= DA_HD) == bool(t), c2, 0.0), axis=-1, keepdims=True),
                            axis=0, keepdims=True)
            ku_ref[t] = jnp.broadcast_to(jnp.sqrt(jnp.maximum(k_lat, k_ctx)), ku_ref.shape[1:])

    def scores_exps(i, t):
        q = q_ref[i * ts:(i + 1) * ts, :].astype(F32) * (DA_HD ** -0.5 * LOG2E)
        qt = jnp.where((lane >= DA_HD) == bool(t), q, 0.0).astype(BF16)
        qf = qt.astype(F32)
        u = jnp.sqrt(jnp.sum(qf * qf, axis=-1, keepdims=True)) * (ku_ref[t, 0:1, 0:1] * DA_BOUND_SLACK)
        l_acc = jnp.zeros((ts, LANES), F32)
        for kc in range(nc):
            p = jnp.exp2(_bdot_nt(qt, k_ref[kc * tk:(kc + 1) * tk, :]) - u)
            p_ref[i, t, kc] = p.astype(BF16)
            l_acc = fold(l_acc, p, jnp.add)
        pc = jnp.exp2(_bdot_nt(qt, ck_ref[...]) - u)
        pc_ref[i, t] = pc.astype(BF16)
        l_acc = fold(l_acc, pc, jnp.add)
        return jnp.sum(l_acc, axis=-1, keepdims=True)

    ls = {}
    for i in range(nsub + 1):
        if i < nsub:
            ls[i, 0] = scores_exps(i, 0)
        if i > 0:
            attend(i - 1, ls[i - 1, 0], ls[i - 1, 1])
        if i < nsub:
            ls[i, 1] = scores_exps(i, 1)
    low = functools.reduce(jnp.minimum, ls.values())
    row_sums_ok = jnp.min(low) >= DA_MIN_L

    @pl.when(jnp.logical_not(row_sums_ok))
    def _():
        exact_step()


def _da(dq, dk, dv, lam_p, subln, layer, ctx_kv):
    b, n, _ = dq.shape
    lam_init = 0.8 - 0.6 * math.exp(-0.3 * layer)
    tq = min(DA_TQ, n)
    ts = min(DA_TS, tq)
    tk = min(DA_TK, n)
    has_ctx = ctx_kv is not None
    qmap = lambda bi, h, qi: (bi, qi, h)
    kvmap = lambda bi, h, qi: (bi, 0, h)
    in_specs = [pl.BlockSpec((4, DA_HD), lambda bi, h, qi: (0, 0)),
                pl.BlockSpec((1, 2 * DA_HD), lambda bi, h, qi: (0, 0)),
                pl.BlockSpec((None, tq, LANES), qmap),
                pl.BlockSpec((None, n, LANES), kvmap),
                pl.BlockSpec((None, n, LANES), kvmap)]
    args = [lam_p, subln.reshape(1, 2 * DA_HD), dq, dk, dv]
    scratch = [pltpu.VMEM((tq // ts, 2, n // tk, ts, tk), F32), pltpu.VMEM((tq // ts, 2, n // tk, ts, tk), BF16)]
    if has_ctx:
        ck, cv, kn = ctx_kv
        past = ck.shape[3]
        cmap = lambda bi, h, qi: (bi, layer, h, 0, 0)
        in_specs += [pl.BlockSpec((None, None, None, past, LANES), cmap)] * 2
        in_specs += [pl.BlockSpec((None, n, LANES), lambda bi, h, qi: (bi, 0, 0))]
        args += [ck, cv, kn]
        scratch += [pltpu.VMEM((tq // ts, 2, ts, past), F32), pltpu.VMEM((tq // ts, 2, ts, past), BF16),
                    pltpu.VMEM((2, SUBLANES, LANES), F32)]
    return pl.pallas_call(
        functools.partial(_da_kernel, lam_init=lam_init, has_ctx=has_ctx, tk=tk),
        grid=(b, DA_H, n // tq),
        in_specs=in_specs,
        out_specs=pl.BlockSpec((None, tq, LANES), qmap),
        out_shape=jax.ShapeDtypeStruct((b, n, DA_W), F32),
        scratch_shapes=scratch,
        compiler_params=_cparams(("arbitrary", "arbitrary", "arbitrary")),
        name="da_lat" if has_ctx else "da_ctx",
    )(*args)


def _pair_attend(q, sources):
    lane = lax.broadcasted_iota(jnp.int32, (1, LANES), 1)
    outs = []
    for j in (0, 1):
        qj = jnp.where((lane >= NA_HD) == bool(j), q, 0.0).astype(BF16)
        ss = []
        for k, _, bias in sources:
            s = _bdot_nt(qj, k)
            if bias is not None:
                s = s + bias[j]
            ss.append(s)
        m = ss[0].max(axis=-1, keepdims=True)
        for s in ss[1:]:
            m = jnp.maximum(m, s.max(axis=-1, keepdims=True))
        l = 0.0
        o = 0.0
        for s, (_, v, _) in zip(ss, sources):
            p = jnp.exp2(s - m)
            l = l + p.sum(axis=-1, keepdims=True)
            o = o + _bdot(p, v)
        outs.append(o * (1.0 / l))
    return jnp.where(lane >= NA_HD, outs[1], outs[0])


def _na_lat_kernel(q_ref, k_ref, v_ref, ck_ref, cv_ref, *rest, n_rows):
    b_refs, o_ref = rest[:NA_SUBS], rest[NA_SUBS]
    step = pl.program_id(2)
    ck, cv = ck_ref[...], cv_ref[...]
    for sub in range(NA_SUBS):
        blk = step * NA_SUBS + sub
        row0 = jnp.clip(blk * NA_QROWS - WIN_R // 2, 0, n_rows - NA_KROWS)
        start = pl.multiple_of(row0 * GRID_W, GRID_W)
        kw = k_ref[pl.ds(start, NA_KW), :]
        vw = v_ref[pl.ds(start, NA_KW), :]
        rows = slice(sub * NA_QB, (sub + 1) * NA_QB)
        q = q_ref[rows, :].astype(F32) * NA_QSCALE
        o_ref[rows, :] = _pair_attend(q, [(kw, vw, b_refs[sub]), (ck, cv, None)])


def _na_ctx_kernel(q_ref, k_ref, v_ref, o_ref):
    q = q_ref[...].astype(F32) * NA_QSCALE
    o_ref[...] = _pair_attend(q, [(k_ref[...], v_ref[...], None)])


def _na_bias_tables(rpb, n_rows):
    nrb = n_rows // NA_QROWS
    nh = rpb.shape[0]
    pad = GRID_W - WIN_C
    rpad = jnp.pad(rpb.astype(F32), ((0, 0), (0, 0), (pad, pad)))
    toep = jnp.stack([rpad[:, :, GRID_W - 1 - qc:2 * GRID_W - 1 - qc] for qc in range(GRID_W)], axis=2)
    cols = np.arange(GRID_W)
    s_c = np.clip(cols - WIN_C // 2, 0, GRID_W - WIN_C)[:, None]
    ok_c = (cols[None, :] >= s_c) & (cols[None, :] < s_c + WIN_C)
    toep = jnp.where(jnp.asarray(ok_c), toep, -jnp.inf)
    masked = jnp.full((nh, GRID_W, GRID_W), -jnp.inf, F32)
    tabs = []
    for rb in (0, 1, nrb - 1):
        row0 = min(max(rb * NA_QROWS - WIN_R // 2, 0), n_rows - NA_KROWS)
        q_rows = rb * NA_QROWS + np.arange(NA_QROWS)
        k_rows = row0 + np.arange(NA_KROWS)
        s_r = np.clip(q_rows - WIN_R // 2, 0, n_rows - WIN_R)[:, None]
        ok_r = (k_rows[None, :] >= s_r) & (k_rows[None, :] < s_r + WIN_R)
        off_r = np.clip(k_rows[None, :] - q_rows[:, None] + WIN_R - 1, 0, 2 * WIN_R - 2)
        rows = [jnp.concatenate([toep[:, off_r[qr, kr]] if ok_r[qr, kr] else masked for kr in range(NA_KROWS)],
                                axis=-1) for qr in range(NA_QROWS)]
        tabs.append(jnp.concatenate(rows, axis=-2) * LOG2E)
    return jnp.stack(tabs, axis=1)


def _na_lat(nq, nk, nv, ck, cv, rpb, layer):
    b, n, _ = nq.shape
    n_rows = n // GRID_W
    assert n_rows >= 2 * NA_KROWS and n_rows % (NA_QROWS * NA_SUBS) == 0 and NA_QROWS == WIN_R // 2
    nrb = n_rows // NA_QROWS
    past = ck.shape[3]
    bias = _na_bias_tables(rpb, n_rows)
    kvmap = lambda bi, hp, st: (bi, 0, hp)
    cmap = lambda bi, hp, st: (bi, layer, hp, 0, 0)
    qmap = lambda bi, hp, st: (bi, st, hp)

    def variant(sub):
        def index(bi, hp, st):
            blk = st * NA_SUBS + sub
            return (hp, jnp.where(blk == 0, 0, jnp.where(blk == nrb - 1, 2, 1)), 0, 0)
        return index

    return pl.pallas_call(
        functools.partial(_na_lat_kernel, n_rows=n_rows),
        grid=(b, NA_H // 2, nrb // NA_SUBS),
        in_specs=[pl.BlockSpec((None, NA_SUBS * NA_QB, LANES), qmap),
                  pl.BlockSpec((None, n, LANES), kvmap),
                  pl.BlockSpec((None, n, LANES), kvmap),
                  pl.BlockSpec((None, None, None, past, LANES), cmap),
                  pl.BlockSpec((None, None, None, past, LANES), cmap)]
        + [pl.BlockSpec((2, None, NA_QB, NA_KW), variant(sub)) for sub in range(NA_SUBS)],
        out_specs=pl.BlockSpec((None, NA_SUBS * NA_QB, LANES), qmap),
        out_shape=jax.ShapeDtypeStruct((b, n, NA_W), F32),
        compiler_params=_cparams(("arbitrary", "arbitrary", "arbitrary")),
        name="na_lat",
    )(nq, nk, nv, ck, cv, *([bias] * NA_SUBS))


def _na_ctx(nq, nk, nv):
    b, n, _ = nq.shape
    spec = pl.BlockSpec((None, n, LANES), lambda bi, hp: (bi, 0, hp))
    return pl.pallas_call(
        _na_ctx_kernel,
        grid=(b, NA_H // 2),
        in_specs=[spec, spec, spec],
        out_specs=spec,
        out_shape=jax.ShapeDtypeStruct((b, n, NA_W), F32),
        compiler_params=_cparams(("arbitrary", "arbitrary")),
        name="na_ctx",
    )(nq, nk, nv)


def _split3(x):
    hi = x.astype(BF16)
    r = x - hi.astype(F32)
    mid = r.astype(BF16)
    return hi, mid, (r - mid.astype(F32)).astype(BF16)


def _dot3_left(x, m01):
    hi, mid, lo = _split3(x)
    d = lambda a: jnp.dot(a, m01, preferred_element_type=F32)
    return (d(lo) + d(mid)) + d(hi)


def _dot3_right(m01, x):
    hi, mid, lo = _split3(x)
    d = lambda a: jnp.dot(m01, a, preferred_element_type=F32)
    return (d(lo) + d(mid)) + d(hi)


def _ssd_block(refs, sb, *, bwd, ns, cps):
    (xm_ref, xp_ref, xn_ref, z_ref, dt_ref, dtt_ref, cw_ref, cb_ref, dtb_ref, dtbt_ref, alogt_ref,
     alogx_ref, dexp_ref, nrm_ref, y_ref, xpad_ref, st_ref, yacc_ref, act_ref) = refs
    t = SSM_CHUNK
    tb = cps * t
    gw = SSM_DI // SSM_G
    hpg = SSM_H // SSM_G
    p_shift = SSM_P.bit_length() - 1
    row0 = pl.multiple_of(sb * tb, tb)

    if not bwd:
        xpad_ref[0:SUBLANES, :] = jnp.where(sb == 0, 0.0, xp_ref[...])
        xpad_ref[SUBLANES:SUBLANES + tb, :] = xm_ref[...]
        xpad_ref[SUBLANES + tb:, :] = jnp.where(sb == ns - 1, 0.0, xn_ref[...])
        conv = cb_ref[...]
        for k in range(CONV_W):
            o = SUBLANES - CONV_W // 2 + k
            conv = conv + cw_ref[k:k + 1, :] * xpad_ref[o:o + tb, :]
        act = _silu(conv)
        act_ref[pl.ds(row0, tb), :] = act
    else:
        act = act_ref[pl.ds(row0, tb), :]
    xs = act[:, :SSM_DI]
    bm = act[:, SSM_DI:SSM_DI + SSM_G * SSM_N]
    cm = act[:, SSM_DI + SSM_G * SSM_N:]

    li = lax.broadcasted_iota(jnp.int32, (t, t), 0)
    si = lax.broadcasted_iota(jnp.int32, (t, t), 1)
    keep = (si >= li) if bwd else (si <= li)
    tri = jnp.where(keep, 1.0, 0.0).astype(BF16)
    trit = jnp.where((li >= si) if bwd else (li <= si), 1.0, 0.0).astype(BF16)
    r_i = lax.broadcasted_iota(jnp.int32, (LANES, SSM_DI), 0)
    c_i = lax.broadcasted_iota(jnp.int32, (LANES, SSM_DI), 1)
    d = int(bwd)
    expand = jnp.where(r_i == d * SSM_H + (c_i >> p_shift), 1.0, 0.0).astype(BF16)
    dtv = _softplus(dt_ref[...] + dtb_ref[...])
    dtx = _dot3_left(dtv, expand)
    a_x = dtx * (-jnp.exp(alogx_ref[d:d + 1, :]))
    xdt = xs * dtx
    at = -jnp.exp(alogt_ref[...]) * _softplus(dtt_ref[...] + dtbt_ref[...])
    lane_g = lax.broadcasted_iota(jnp.int32, (1, gw), 1)

    order = range(cps - 1, -1, -1) if bwd else range(cps)
    y_rows = [None] * cps
    for ci in order:
        rows = slice(ci * t, (ci + 1) * t)
        acum = _dot3_right(tri, a_x[rows])
        tot = acum[0:1, :] if bwd else acum[t - 1:t, :]
        e_in = jnp.exp(acum)
        xdt_c = xdt[rows]
        xdt_b = xdt_c.astype(BF16)
        xout_b = (xdt_c * jnp.exp(tot - acum)).astype(BF16)
        acum_t = _dot3_left(at[:, rows], trit)
        y_parts = []
        for g in range(SSM_G):
            cg = cm[rows, g * SSM_N:(g + 1) * SSM_N].astype(BF16)
            bg = bm[rows, g * SSM_N:(g + 1) * SSM_N]
            gmat = _bdot_nt(cg, bg)
            st_g = st_ref[:, g * gw:(g + 1) * gw]
            y_g = _bdot(cg, st_g) * e_in[:, g * gw:(g + 1) * gw]
            for hh in range(hpg):
                h = g * hpg + hh
                col = acum[:, h * SSM_P:h * SSM_P + 1]
                row = acum_t[d * SSM_H + h:d * SSM_H + h + 1, :]
                lmat = jnp.exp(jnp.where(keep, col - row, -jnp.inf))
                yh = _bdot(gmat * lmat, xdt_b[:, g * gw:(g + 1) * gw])
                y_g = y_g + jnp.where((lane_g >> p_shift) == hh, yh, 0.0)
            y_parts.append(y_g)
            st_ref[:, g * gw:(g + 1) * gw] = (st_g * jnp.exp(tot[:, g * gw:(g + 1) * gw])
                                              + _bdot(bg.T, xout_b[:, g * gw:(g + 1) * gw]))
        y_rows[ci] = jnp.concatenate(y_parts, axis=-1)
    y_dir = jnp.concatenate(y_rows, axis=0)

    if not bwd:
        yacc_ref[pl.ds(row0, tb), :] = y_dir + xs * dexp_ref[...]
    else:
        y = (yacc_ref[pl.ds(row0, tb), :] + y_dir) * _silu(z_ref[...])
        parts = [_rms(y[:, g * gw:(g + 1) * gw], nrm_ref[:, g * gw:(g + 1) * gw]) for g in range(SSM_G)]
        y_ref[...] = jnp.concatenate(parts, axis=-1)


def _ssd_kernel(*refs, ns, cps, has_h0):
    n_in = 14
    if has_h0:
        h0f_ref, h0b_ref = refs[n_in:n_in + 2]
        refs = refs[:n_in] + refs[n_in + 2:]
    hf_ref, hb_ref = refs[n_in + 1:n_in + 3]
    block_refs = refs[:n_in + 1] + refs[n_in + 3:]
    st_ref = block_refs[n_in + 2]
    j = pl.program_id(1)

    @pl.when(j == 0)
    def _():
        st_ref[...] = h0f_ref[...] if has_h0 else jnp.zeros_like(st_ref)

    @pl.when(j == ns)
    def _():
        st_ref[...] = h0b_ref[...] if has_h0 else jnp.zeros_like(st_ref)

    @pl.when(j < ns)
    def _():
        _ssd_block(block_refs, j, bwd=False, ns=ns, cps=cps)

    @pl.when(j >= ns)
    def _():
        _ssd_block(block_refs, 2 * ns - 1 - j, bwd=True, ns=ns, cps=cps)

    @pl.when(j == ns - 1)
    def _():
        hf_ref[...] = st_ref[...]

    @pl.when(j == 2 * ns - 1)
    def _():
        hb_ref[...] = st_ref[...]


def _ssd(sz, sxbc, sdt, p, h0):
    b, n, _ = sz.shape
    t = SSM_CHUNK
    cps = min(SSD_CPS, n // t)
    tb = cps * t
    assert n % tb == 0
    ns = n // tb
    nb8 = n // SUBLANES
    tb8 = tb // SUBLANES
    has_h0 = h0 is not None
    sdt_t = jnp.swapaxes(sdt[:, :, :2 * SSM_H], 1, 2)
    sidx = lambda j: jnp.where(j >= ns, 2 * ns - 1 - j, j)
    fidx = lambda j: jnp.minimum(j, ns - 1)
    zidx = lambda j: jnp.where(j >= ns, 2 * ns - 1 - j, ns - 1)
    const2 = lambda bi, j: (0, 0)
    cw = jnp.zeros((SUBLANES, CONV_CH), F32).at[:CONV_W].set(p["ssm_conv_w"])
    dtb = jnp.zeros((1, DT_PAD), F32).at[0, :2 * SSM_H].set(p["ssm_dt_bias"].reshape(-1))
    dtbt = jnp.broadcast_to(p["ssm_dt_bias"].reshape(2 * SSM_H, 1), (2 * SSM_H, tb))
    alogt = jnp.broadcast_to(p["ssm_a_log"].reshape(2 * SSM_H, 1), (2 * SSM_H, tb))
    alogx = jnp.repeat(p["ssm_a_log"], SSM_P, axis=-1)
    dexp = jnp.repeat(p["ssm_d"], SSM_P).reshape(1, SSM_DI)
    in_specs = [pl.BlockSpec((None, tb, CONV_CH), lambda bi, j: (bi, fidx(j), 0)),
                pl.BlockSpec((None, SUBLANES, CONV_CH), lambda bi, j: (bi, jnp.maximum(fidx(j) * tb8 - 1, 0), 0)),
                pl.BlockSpec((None, SUBLANES, CONV_CH),
                             lambda bi, j: (bi, jnp.minimum((fidx(j) + 1) * tb8, nb8 - 1), 0)),
                pl.BlockSpec((None, tb, SSM_DI), lambda bi, j: (bi, zidx(j), 0)),
                pl.BlockSpec((None, tb, DT_PAD), lambda bi, j: (bi, sidx(j), 0)),
                pl.BlockSpec((None, 2 * SSM_H, tb), lambda bi, j: (bi, 0, sidx(j))),
                pl.BlockSpec((SUBLANES, CONV_CH), const2),
                pl.BlockSpec((1, CONV_CH), const2),
                pl.BlockSpec((1, DT_PAD), const2),
                pl.BlockSpec((2 * SSM_H, tb), const2),
                pl.BlockSpec((2 * SSM_H, tb), const2),
                pl.BlockSpec((2, SSM_DI), const2),
                pl.BlockSpec((1, SSM_DI), const2),
                pl.BlockSpec((1, SSM_DI), const2)]
    args = [sxbc, sxbc, sxbc, sz, sdt, sdt_t, cw, p["ssm_conv_b"].reshape(1, CONV_CH), dtb, dtbt,
            alogt, alogx, dexp, p["ssm_norm"].reshape(1, SSM_DI)]
    st_spec = pl.BlockSpec((None, SSM_N, SSM_DI), lambda bi, j: (bi, 0, 0))
    if has_h0:
        in_specs += [st_spec, st_spec]
        args += list(h0)
    y, hf, hb = pl.pallas_call(
        functools.partial(_ssd_kernel, ns=ns, cps=cps, has_h0=has_h0),
        grid=(b, 2 * ns),
        in_specs=in_specs,
        out_specs=[pl.BlockSpec((None, tb, SSM_DI), lambda bi, j: (bi, zidx(j), 0)), st_spec, st_spec],
        out_shape=[jax.ShapeDtypeStruct((b, n, SSM_DI), F32),
                   jax.ShapeDtypeStruct((b, SSM_N, SSM_DI), F32),
                   jax.ShapeDtypeStruct((b, SSM_N, SSM_DI), F32)],
        scratch_shapes=[pltpu.VMEM((tb + 2 * SUBLANES, CONV_CH), F32),
                        pltpu.VMEM((SSM_N, SSM_DI), F32),
                        pltpu.VMEM((n, SSM_DI), F32),
                        pltpu.VMEM((n, CONV_CH), F32)],
        compiler_params=_cparams(("arbitrary", "arbitrary")),
        name="ssd_lat" if has_h0 else "ssd_ctx",
    )(*args)
    return y, hf, hb


def _post_kernel(x_ref, oda_ref, ossm_ref, ona_ref, gate_ref, mod_ref, wda_ref, wssm_ref, wna_ref, wout_ref,
                 gpost_ref, gpre_ref, wr_ref, br_ref, x1_ref, h2_ref, rg_ref):
    d = x_ref.shape[1]
    tm = x_ref.shape[0]
    ts = min(POST_TS, tm)
    for i in range(tm // ts):
        r = slice(i * ts, (i + 1) * ts)
        mix = (gate_ref[r, 0:d] * _bdot(oda_ref[r, :], wda_ref[...])
               + gate_ref[r, d:2 * d] * _bdot(ossm_ref[r, :], wssm_ref[...])
               + gate_ref[r, 2 * d:3 * d] * _bdot(ona_ref[r, :], wna_ref[...]))
        mix = _bdot(mix, wout_ref[...])
        x1 = x_ref[r, :] + mod_ref[2:3, :] * _rms(mix, gpost_ref[...])
        x1_ref[r, :] = x1
        h2 = _rms(x1, gpre_ref[...]) * (1.0 + mod_ref[4:5, :]) + mod_ref[3:4, :]
        h_hi = h2.astype(BF16)
        h2_ref[r, :] = h_hi

        h_mid = (h2 - h_hi.astype(F32)).astype(BF16)
        a = jnp.dot(h_hi, wr_ref[...], preferred_element_type=F32)
        b = jnp.dot(h_mid, wr_ref[:, :LANES], preferred_element_type=F32)
        s = _sigmoid(a[:, :LANES] + (a[:, LANES:] + b))
        work = s + br_ref[...]
        col = lax.broadcasted_iota(jnp.int32, (ts, LANES), 1).astype(F32)
        picked = jnp.zeros((ts, LANES), F32)
        for _ in range(TOP_K):
            m = jnp.max(work, axis=-1, keepdims=True)
            first = jnp.min(jnp.where(work == m, col, float(LANES)), axis=-1, keepdims=True)
            hit = col == first
            picked = jnp.where(hit, 1.0, picked)
            work = jnp.where(hit, -jnp.inf, work)
        sel = picked * s
        g = sel / jnp.sum(sel, axis=-1, keepdims=True) * ROUTE_SCALE
        g_hi = g.astype(BF16).astype(F32)
        r1 = g - g_hi
        g_mid = r1.astype(BF16).astype(F32)
        g_lo = r1 - g_mid
        packed = g_hi + pltpu.roll(g_mid, N_EXPERTS, 1) + pltpu.roll(g_lo, 2 * N_EXPERTS, 1)
        rg_ref[r, :] = packed.astype(rg_ref.dtype)


def _post(x3, o_da, o_ssm, o_na, gates, mods, p):
    b, n, d = x3.shape
    tm = min(POST_TM, n)
    tps = n // tm
    t = b * n
    per_batch_mods = mods.shape[0] > 1
    mod_idx = (lambda i: (i // tps, 0, 0)) if per_batch_mods else (lambda i: (0, 0, 0))
    row = lambda i: (i, 0)
    const2 = lambda i: (0, 0)
    flat = lambda a: a.reshape(t, a.shape[-1])
    wspec = lambda k: pl.BlockSpec((k, d), const2)
    w_r = jnp.zeros((d, LANES), F32).at[:, :N_EXPERTS].set(p["w_router"])
    w_r_hi = w_r.astype(BF16)
    w_router = jnp.concatenate([w_r_hi, (w_r - w_r_hi.astype(F32)).astype(BF16)], axis=1)
    b_router = jnp.full((1, LANES), -jnp.inf, F32).at[0, :N_EXPERTS].set(p["b_router"])
    x1, h2, route = pl.pallas_call(
        _post_kernel,
        grid=(t // tm,),
        in_specs=[pl.BlockSpec((tm, d), row), pl.BlockSpec((tm, DA_W), row), pl.BlockSpec((tm, SSM_DI), row),
                  pl.BlockSpec((tm, NA_W), row), pl.BlockSpec((tm, 3 * d), row),
                  pl.BlockSpec((None, 6, d), mod_idx),
                  wspec(DA_W), wspec(SSM_DI), wspec(NA_W), wspec(d),
                  pl.BlockSpec((1, d), const2), pl.BlockSpec((1, d), const2),
                  pl.BlockSpec((d, 2 * LANES), const2), pl.BlockSpec((1, LANES), const2)],
        out_specs=[pl.BlockSpec((tm, d), row), pl.BlockSpec((tm, d), row), pl.BlockSpec((tm, LANES), row)],
        out_shape=[jax.ShapeDtypeStruct((t, d), F32), jax.ShapeDtypeStruct((t, d), BF16),
                   jax.ShapeDtypeStruct((t, LANES), BF16)],
        compiler_params=_cparams(("arbitrary",)),
        name="post",
    )(flat(x3), flat(o_da), flat(o_ssm), flat(o_na), flat(gates), mods,
      p["w_br_da"], p["w_br_ssm"], p["w_br_na"], p["w_out"],
      p["g_post1"].reshape(1, d), p["g_pre2"].reshape(1, d), w_router, b_router)
    return x1, h2, route


def _moe_kernel(x1_ref, h2_ref, rg_ref, mod_ref, wsg_ref, wsu_ref, wsd_ref, weg_ref, weu_ref, wed_ref,
                gpost_ref, o_ref, acc_ref):
    e = pl.program_id(1)
    hb = h2_ref[...]

    @pl.when(e == 0)
    def _():
        acc_ref[...] = _bdot(_silu(_bdot(hb, wsg_ref[...])) * _bdot(hb, wsu_ref[...]), wsd_ref[...])

    width = MOE_EG * EXPERT_FF
    r_i = lax.broadcasted_iota(jnp.int32, (LANES, width), 0)
    c_i = lax.broadcasted_iota(jnp.int32, (LANES, width), 1)
    ff_shift = EXPERT_FF.bit_length() - 1
    onehot = jnp.where((r_i < 3 * N_EXPERTS) & ((r_i & (N_EXPERTS - 1)) == e * MOE_EG + (c_i >> ff_shift)),
                       1.0, 0.0).astype(BF16)
    g_exp = jnp.dot(rg_ref[...], onehot, preferred_element_type=F32)
    act = _silu(_bdot(hb, weg_ref[...])) * _bdot(hb, weu_ref[...]) * g_exp
    acc_ref[...] += _bdot(act, wed_ref[...])

    @pl.when(e == pl.num_programs(1) - 1)
    def _():
        o_ref[...] = x1_ref[...] + mod_ref[5:6, :] * _rms(acc_ref[...], gpost_ref[...])


def _moe(x1, h2, route, mods, p, n):
    t, d = x1.shape
    tm = MOE_TM
    per_batch_mods = mods.shape[0] > 1
    if per_batch_mods:
        assert n % tm == 0
    tps = max(n // tm, 1)
    mod_idx = (lambda i, e: (i // tps, 0, 0)) if per_batch_mods else (lambda i, e: (0, 0, 0))
    row = lambda i, e: (i, 0)
    const2 = lambda i, e: (0, 0)
    width = MOE_EG * EXPERT_FF
    resident = dict(pipeline_mode=pl.Buffered(1)) if MOE_EG == N_EXPERTS else {}
    return pl.pallas_call(
        _moe_kernel,
        grid=(t // tm, N_EXPERTS // MOE_EG),
        in_specs=[pl.BlockSpec((tm, d), row), pl.BlockSpec((tm, d), row), pl.BlockSpec((tm, LANES), row),
                  pl.BlockSpec((None, 6, d), mod_idx),
                  pl.BlockSpec((d, SHARED_FF), const2), pl.BlockSpec((d, SHARED_FF), const2),
                  pl.BlockSpec((SHARED_FF, d), const2),
                  pl.BlockSpec((d, width), lambda i, e: (0, e), **resident),
                  pl.BlockSpec((d, width), lambda i, e: (0, e), **resident),
                  pl.BlockSpec((width, d), lambda i, e: (e, 0), **resident),
                  pl.BlockSpec((1, d), const2)],
        out_specs=pl.BlockSpec((tm, d), row),
        out_shape=jax.ShapeDtypeStruct((t, d), F32),
        scratch_shapes=[pltpu.VMEM((tm, d), F32)],
        compiler_params=_cparams(("arbitrary", "arbitrary")),
        name="moe",
    )(x1, h2, route, mods,
      p["w_s_gate"], p["w_s_up"], p["w_s_down"], p["w_e_gate"], p["w_e_up"], p["w_e_down"],
      p["g_post2"].reshape(1, d))


def _rope_tables(n):
    pos = jnp.arange(n)
    rows = (pos // GRID_W).astype(F32)
    cols = (pos % GRID_W).astype(F32)
    n_freq = DA_HD // 4
    inv = ROPE_BASE ** (-jnp.arange(n_freq, dtype=F32) / n_freq)
    ar, ac = rows[:, None] * inv, cols[:, None] * inv
    zero = jnp.zeros_like(ar)

    def lanes(first_r, second_r, first_c, second_c):
        half = jnp.concatenate([first_r, second_r, first_c, second_c], axis=-1)
        return jnp.concatenate([half, half], axis=-1)

    cos = lanes(jnp.cos(ar), jnp.cos(ar), jnp.cos(ac), jnp.cos(ac))
    s_first = lanes(-jnp.sin(ar), zero, -jnp.sin(ac), zero)
    s_second = lanes(zero, jnp.sin(ar), zero, jnp.sin(ac))
    return cos, s_first, s_second


def _prep_layer_params(stacked, layer):
    p = {name: arr[layer] for name, arr in stacked.items()}
    w_in = p["w_in"]
    dt0 = 3 * DA_W + SSM_DI + CONV_CH
    d = w_in.shape[0]
    p["w_in_p"] = jnp.concatenate(
        [w_in[:, :dt0], w_in[:, dt0 + 2 * SSM_H:], w_in[:, dt0:dt0 + 2 * SSM_H],
         jnp.zeros((d, DT_PAD - 2 * SSM_H), w_in.dtype)], axis=-1).astype(BF16)
    for name in ("w_gate", "w_br_da", "w_br_ssm", "w_br_na", "w_out", "w_s_gate", "w_s_up", "w_s_down"):
        p[name] = p[name].astype(BF16)
    ff = N_EXPERTS * EXPERT_FF
    p["w_e_gate"] = p["w_e_gate"].astype(BF16).transpose(1, 0, 2).reshape(d, ff)
    p["w_e_up"] = p["w_e_up"].astype(BF16).transpose(1, 0, 2).reshape(d, ff)
    p["w_e_down"] = p["w_e_down"].astype(BF16).reshape(ff, d)
    return p


def _trunk_layer(x3, mods, p, layer, ctx):
    b, n, d = x3.shape
    latent = ctx is not None
    rope_tabs = _rope_tables(n) if latent else None
    qkv_dtype = BF16 if latent else F32
    outs = _pre(x3, mods, p["g_pre1"], p["w_in_p"], p["w_gate"], rope_tabs, qkv_dtype)
    dq, dk, dv, sz, sxbc, nq, nk, nv, sdt, gates = outs[:10]
    if latent:
        ck, cv, cnk, cnv, h0f, h0b = ctx
        o_da = _da(dq, dk, dv, p["da_lambda"], p["da_subln"], layer, (ck, cv, outs[10]))
        o_na = _na_lat(nq, nk, nv, cnk, cnv, p["na_rpb"], layer)
        o_ssm, hf, hb = _ssd(sz, sxbc, sdt, p, (h0f, h0b))
    else:
        o_da = _da(dq, dk, dv, p["da_lambda"], p["da_subln"], layer, None)
        o_na = _na_ctx(nq, nk, nv)
        o_ssm, hf, hb = _ssd(sz, sxbc, sdt, p, None)
    x1, h2, route = _post(x3, o_da, o_ssm, o_na, gates, mods, p)
    x2 = _moe(x1, h2, route, mods, p, n).reshape(b, n, d)
    state = None
    if not latent:
        to_state = lambda s: s.reshape(b, SSM_N, SSM_H, SSM_P).transpose(0, 2, 3, 1)
        state = (dk.reshape(b, n, DA_H, 2, DA_HD).transpose(0, 2, 1, 3, 4),
                 dv.reshape(b, n, DA_H, 2 * DA_HD).transpose(0, 2, 1, 3),
                 nk.reshape(b, n, NA_H, NA_HD).transpose(0, 2, 1, 3),
                 nv.reshape(b, n, NA_H, NA_HD).transpose(0, 2, 1, 3),
                 to_state(hf), to_state(hb))
    return x2, state


def kernel(x_prompt, x_sample, c, cache_da_k, cache_da_v, cache_na_k, cache_na_v, state_ssm_f, state_ssm_b,
           c_ctx, w_ada, b_ada, g_pre1, g_post1, g_pre2, g_post2, w_in, w_gate, da_lambda, da_subln,
           ssm_conv_w, ssm_conv_b, ssm_dt_bias, ssm_a_log, ssm_d, ssm_norm, na_rpb, w_br_da, w_br_ssm,
           w_br_na, w_out, w_router, b_router, w_e_gate, w_e_up, w_e_down, w_s_gate, w_s_up, w_s_down):
    stacked = {
        "g_pre1": g_pre1, "g_post1": g_post1, "g_pre2": g_pre2, "g_post2": g_post2, "w_in": w_in,
        "w_gate": w_gate, "da_lambda": da_lambda, "da_subln": da_subln, "ssm_conv_w": ssm_conv_w,
        "ssm_conv_b": ssm_conv_b, "ssm_dt_bias": ssm_dt_bias, "ssm_a_log": ssm_a_log, "ssm_d": ssm_d,
        "ssm_norm": ssm_norm, "na_rpb": na_rpb, "w_br_da": w_br_da, "w_br_ssm": w_br_ssm, "w_br_na": w_br_na,
        "w_out": w_out, "w_router": w_router, "b_router": b_router, "w_e_gate": w_e_gate, "w_e_up": w_e_up,
        "w_e_down": w_e_down, "w_s_gate": w_s_gate, "w_s_up": w_s_up, "w_s_down": w_s_down,
    }
    depth = w_ada.shape[0]
    nb = x_sample.shape[0]
    d = x_prompt.shape[-1]
    rows = -(-(1 + nb) // SUBLANES) * SUBLANES
    cvec = jnp.zeros((rows, d), F32).at[0].set(c_ctx).at[1:1 + nb].set(c)
    mods = _ada(cvec, w_ada, b_ada).reshape(depth, rows, 6, d)

    past = cache_da_k.shape[3]
    ck_all = cache_da_k.reshape(nb, depth, DA_H, past, 2 * DA_HD)
    pair = lambda a: a.reshape(nb, depth, NA_H // 2, 2, past, NA_HD).transpose(0, 1, 2, 4, 3, 5).reshape(
        nb, depth, NA_H // 2, past, 2 * NA_HD)
    cnk_all, cnv_all = pair(cache_na_k), pair(cache_na_v)
    st_t = lambda s: s.transpose(0, 1, 4, 2, 3).reshape(nb, depth, SSM_N, SSM_DI)
    h0f_all, h0b_all = st_t(state_ssm_f), st_t(state_ssm_b)

    xp, xs = x_prompt, x_sample
    new = ([], [], [], [], [], [])
    for layer in range(depth):
        p = _prep_layer_params(stacked, layer)
        xp, st = _trunk_layer(xp, mods[layer, 0:1], p, layer, None)
        for lst, arr in zip(new, st):
            lst.append(arr)
        ctx = (ck_all, cache_da_v, cnk_all, cnv_all, h0f_all[:, layer], h0b_all[:, layer])
        xs, _ = _trunk_layer(xs, mods[layer, 1:1 + nb], p, layer, ctx)
    return (xp, xs) + tuple(jnp.stack(lst, axis=1) for lst in new)
```

```python
import functools
import math

import numpy as np
import jax
import jax.numpy as jnp
from jax import lax
from jax.experimental import pallas as pl
from jax.experimental.pallas import tpu as pltpu

F32 = jnp.float32
BF16 = jnp.bfloat16

D_MODEL = 1024
GRID_W = 64
EPS = 1e-6
DA_H, DA_HD = 4, 64
DA_W = DA_H * 2 * DA_HD
ROPE_BASE = 10000.0
SSM_H, SSM_P, SSM_G, SSM_N = 8, 64, 2, 128
SSM_DI = SSM_H * SSM_P
SSM_CHUNK = 128
CONV_W = 5
CONV_CH = SSM_DI + 2 * SSM_G * SSM_N
NA_H, NA_HD = 8, 64
NA_W = NA_H * NA_HD
WIN_R, WIN_C = 8, 16
N_EXPERTS, TOP_K, EXPERT_FF, SHARED_FF = 32, 4, 128, 256
ROUTE_SCALE = 2.5

LANES = 128
SUBLANES = 8
VMEM_LIMIT_BYTES = 56 * 2**20

PRE_TM = 256
POST_TM = 512
POST_TS = 512
MOE_TM = 256
MOE_EG = 32
DA_TQ = 512
DA_TS = 256
SSD_CPS = 4
DA_BOUND_SLACK = 1.0 + 2.0 ** -6
DA_MIN_L = 2.0 ** -60
DA_TK = 512
NA_QROWS = WIN_R // 2
NA_QB = NA_QROWS * GRID_W
NA_SUBS = 4
NA_KROWS = NA_QROWS + WIN_R
NA_KW = NA_KROWS * GRID_W
DT_PAD = LANES

OFF_DQ, OFF_DK, OFF_DV, OFF_SZ, OFF_XBC, OFF_NQ, OFF_NK, OFF_NV, OFF_DT = (
    0, 512, 1024, 1536, 2048, 3072, 3584, 4096, 4608)
IN_WP = OFF_DT + DT_PAD

_NT = (((1,), (1,)), ((), ()))
LOG2E = math.log2(math.e)
NA_QSCALE = NA_HD ** -0.5 * LOG2E


def _cparams(sem):
    return pltpu.CompilerParams(dimension_semantics=sem, vmem_limit_bytes=VMEM_LIMIT_BYTES)


def _bdot(a, b):
    return jnp.dot(a.astype(BF16), b.astype(BF16), preferred_element_type=F32)


def _bdot_nt(a, b):
    return lax.dot_general(a.astype(BF16), b.astype(BF16), _NT, preferred_element_type=F32)


def _fdot(a, b):
    return jnp.dot(a, b, preferred_element_type=F32, precision=lax.Precision.HIGHEST)


def _sigmoid(x):
    return 1.0 / (1.0 + jnp.exp(-x))


def _silu(x):
    return x * _sigmoid(x)


def _softplus(x):
    return jnp.maximum(x, 0.0) + jnp.log(1.0 + jnp.exp(-jnp.abs(x)))


def _rms(x, g):
    return x * lax.rsqrt(jnp.mean(x * x, axis=-1, keepdims=True) + EPS) * g


def _ada_kernel(c_ref, w_ref, b_ref, o_ref):
    o_ref[...] = _bdot(_silu(c_ref[...]), w_ref[...]) + b_ref[...]


def _ada(cvec, w_ada, b_ada):
    nl, d, d6 = w_ada.shape
    r = cvec.shape[0]
    tn = 1024
    return pl.pallas_call(
        _ada_kernel,
        grid=(nl, d6 // tn),
        in_specs=[pl.BlockSpec((r, d), lambda l, j: (0, 0)),
                  pl.BlockSpec((None, d, tn), lambda l, j: (l, 0, j)),
                  pl.BlockSpec((None, 1, tn), lambda l, j: (l, 0, j))],
        out_specs=pl.BlockSpec((None, r, tn), lambda l, j: (l, 0, j)),
        out_shape=jax.ShapeDtypeStruct((nl, r, d6), F32),
        compiler_params=_cparams(("arbitrary", "arbitrary")),
        name="ada",
    )(cvec, w_ada, b_ada.reshape(nl, 1, d6))


def _pre_kernel(*refs, rope):
    if rope:
        x_ref, mod_ref, g_ref, win_ref, wg_ref, cos_ref, sa_ref, sb_ref = refs[:8]
        outs = refs[8:]
    else:
        x_ref, mod_ref, g_ref, win_ref, wg_ref = refs[:5]
        outs = refs[5:]
    dq_ref, dk_ref, dv_ref, sz_ref, xbc_ref, nq_ref, nk_ref, nv_ref, dt_ref, gate_ref = outs[:10]
    x = x_ref[...]
    h = _rms(x, g_ref[...]) * (1.0 + mod_ref[1:2, :]) + mod_ref[0:1, :]
    hb = h.astype(BF16)

    def proj(off, width):
        return jnp.dot(hb, win_ref[:, off:off + width], preferred_element_type=F32)

    lane = lax.broadcasted_iota(jnp.int32, (1, LANES), 1)
    for off, ref in ((OFF_DQ, dq_ref), (OFF_DK, dk_ref)):
        if rope:
            cos, sa, sb = cos_ref[...], sa_ref[...], sb_ref[...]
            kn = jnp.zeros((x.shape[0], LANES), F32)
            for hd in range(DA_H):
                u = proj(off + hd * LANES, LANES)
                u = u * cos + pltpu.roll(u, LANES - 16, 1) * sa + pltpu.roll(u, 16, 1) * sb
                ub = u.astype(ref.dtype)
                ref[:, hd * LANES:(hd + 1) * LANES] = ub
                if ref is dk_ref:
                    sq = ub.astype(F32) * ub.astype(F32)
                    for t in (0, 1):
                        nt = jnp.sum(jnp.where((lane >= DA_HD) == bool(t), sq, 0.0), axis=-1, keepdims=True)
                        kn = jnp.where(lane == 2 * hd + t, nt, kn)
            if ref is dk_ref:
                outs[10][...] = kn
        else:
            ref[...] = proj(off, DA_W).astype(ref.dtype)
    dv_ref[...] = proj(OFF_DV, DA_W).astype(dv_ref.dtype)
    sz_ref[...] = proj(OFF_SZ, SSM_DI)
    xbc_ref[...] = proj(OFF_XBC, CONV_CH)
    nq_ref[...] = proj(OFF_NQ, NA_W).astype(nq_ref.dtype)
    nk_ref[...] = proj(OFF_NK, NA_W).astype(nk_ref.dtype)
    nv_ref[...] = proj(OFF_NV, NA_W).astype(nv_ref.dtype)
    dt_ref[...] = proj(OFF_DT, DT_PAD)
    gate_ref[...] = _sigmoid(jnp.dot(hb, wg_ref[...], preferred_element_type=F32))


def _pre(x3, mods, g_pre1, w_in_p, w_gate, rope_tabs, qkv_dtype):
    b, n, d = x3.shape
    tm = PRE_TM
    tps = n // tm
    t = b * n
    x2 = x3.reshape(t, d)
    per_batch_mods = mods.shape[0] > 1
    mod_idx = (lambda i: (i // tps, 0, 0)) if per_batch_mods else (lambda i: (0, 0, 0))
    const2 = lambda i: (0, 0)
    row = lambda i: (i, 0)
    in_specs = [pl.BlockSpec((tm, d), row),
                pl.BlockSpec((None, 6, d), mod_idx),
                pl.BlockSpec((1, d), const2),
                pl.BlockSpec((d, IN_WP), const2, pipeline_mode=pl.Buffered(1)),
                pl.BlockSpec((d, 3 * d), const2, pipeline_mode=pl.Buffered(1))]
    args = [x2, mods, g_pre1.reshape(1, d), w_in_p, w_gate]
    rope = rope_tabs is not None
    if rope:
        in_specs += [pl.BlockSpec((tm, LANES), lambda i: (i % tps, 0))] * 3
        args += list(rope_tabs)
    widths = (DA_W, DA_W, DA_W, SSM_DI, CONV_CH, NA_W, NA_W, NA_W, DT_PAD, 3 * d)
    dtypes = (qkv_dtype, qkv_dtype, qkv_dtype, F32, F32, qkv_dtype, qkv_dtype, qkv_dtype, F32, F32)
    if rope:
        widths += (LANES,)
        dtypes += (F32,)
    outs = pl.pallas_call(
        functools.partial(_pre_kernel, rope=rope),
        grid=(t // tm,),
        in_specs=in_specs,
        out_specs=[pl.BlockSpec((tm, w), row) for w in widths],
        out_shape=[jax.ShapeDtypeStruct((t, w), dt) for w, dt in zip(widths, dtypes)],
        compiler_params=_cparams(("arbitrary",)),
        name="pre_rope" if rope else "pre",
    )(*args)
    return [o.reshape(b, n, o.shape[-1]) for o in outs]


def _da_kernel(*refs, lam_init, has_ctx, tk):
    if has_ctx:
        (lam_ref, sub_ref, q_ref, k_ref, v_ref, ck_ref, cv_ref, kn_ref, o_ref, s_ref, p_ref, sc_ref, pc_ref,
         ku_ref) = refs
    else:
        lam_ref, sub_ref, q_ref, k_ref, v_ref, o_ref, s_ref, p_ref = refs
    nsub, _, _, ts, _ = s_ref.shape
    nc = k_ref.shape[0] // tk
    lf = lam_ref[...]
    lam = (jnp.exp(jnp.sum(lf[0:1] * lf[1:2], axis=-1, keepdims=True))
           - jnp.exp(jnp.sum(lf[2:3] * lf[3:4], axis=-1, keepdims=True)) + lam_init)
    lane = lax.broadcasted_iota(jnp.int32, (1, LANES), 1)

    def fold(acc, x, op):
        for j in range(x.shape[1] // LANES):
            acc = op(acc, x[:, j * LANES:(j + 1) * LANES])
        return acc

    def scores(i, t):
        q = q_ref[i * ts:(i + 1) * ts, :].astype(F32) * (DA_HD ** -0.5 * LOG2E)
        qt = jnp.where((lane >= DA_HD) == bool(t), q, 0.0).astype(BF16)
        m_acc = jnp.full((ts, LANES), -jnp.inf, F32)
        for kc in range(nc):
            s = _bdot_nt(qt, k_ref[kc * tk:(kc + 1) * tk, :])
            s_ref[i, t, kc] = s
            m_acc = fold(m_acc, s, jnp.maximum)
        if has_ctx:
            sc = _bdot_nt(qt, ck_ref[...])
            sc_ref[i, t] = sc
            m_acc = fold(m_acc, sc, jnp.maximum)
        return jnp.max(m_acc, axis=-1, keepdims=True)

    def exps(i, t, m):
        l_acc = jnp.zeros((ts, LANES), F32)
        for kc in range(nc):
            p = jnp.exp2(s_ref[i, t, kc] - m)
            p_ref[i, t, kc] = p.astype(BF16)
            l_acc = fold(l_acc, p, jnp.add)
        if has_ctx:
            pc = jnp.exp2(sc_ref[i, t] - m)
            pc_ref[i, t] = pc.astype(BF16)
            l_acc = fold(l_acc, pc, jnp.add)
        return jnp.sum(l_acc, axis=-1, keepdims=True)

    def attend(i, l0, l1):
        r = (lam * l0 / l1).astype(BF16)
        acc = jnp.zeros((ts, LANES), F32)
        for kc in range(nc):
            acc = acc + jnp.dot(p_ref[i, 0, kc] - p_ref[i, 1, kc] * r,
                                v_ref[kc * tk:(kc + 1) * tk, :].astype(BF16), preferred_element_type=F32)
        if has_ctx:
            acc = acc + jnp.dot(pc_ref[i, 0] - pc_ref[i, 1] * r, cv_ref[...].astype(BF16),
                                preferred_element_type=F32)
        o = _rms(acc * (1.0 / l0), sub_ref[...]) * (1.0 - lam_init)
        o_ref[i * ts:(i + 1) * ts, :] = o.astype(o_ref.dtype)

    def exact_step():
        m, inv = {}, {}
        for i in range(nsub + 1):
            if i < nsub:
                m[i, 0] = scores(i, 0)
            if i > 0:
                inv[i - 1, 1] = exps(i - 1, 1, m[i - 1, 1])
            if i < nsub:
                m[i, 1] = scores(i, 1)
            if i > 0:
                attend(i - 1, inv[i - 1, 0], inv[i - 1, 1])
            if i < nsub:
                inv[i, 0] = exps(i, 0, m[i, 0])

    if not has_ctx:
        exact_step()
        return

    @pl.when(pl.program_id(2) == 0)
    def _():
        hd = pl.program_id(1)
        kmax = jnp.max(kn_ref[...], axis=0, keepdims=True)
        ckf = ck_ref[...].astype(BF16).astype(F32)
        c2 = ckf * ckf
        for t in (0, 1):
            k_lat = jnp.max(jnp.where(lane == 2 * hd + t, kmax, 0.0), axis=-1, keepdims=True)
            k_ctx = jnp.max(jnp.sum(jnp.where((lane >= DA_HD) == bool(t), c2, 0.0), axis=-1, keepdims=True),
                            axis=0, keepdims=True)
            ku_ref[t] = jnp.broadcast_to(jnp.sqrt(jnp.maximum(k_lat, k_ctx)), ku_ref.shape[1:])

    def scores_exps(i, t):
        q = q_ref[i * ts:(i + 1) * ts, :].astype(F32) * (DA_HD ** -0.5 * LOG2E)
        qt = jnp.where((lane >= DA_HD) == bool(t), q, 0.0).astype(BF16)
        qf = qt.astype(F32)
        u = jnp.sqrt(jnp.sum(qf * qf, axis=-1, keepdims=True)) * (ku_ref[t, 0:1, 0:1] * DA_BOUND_SLACK)
        l_acc = jnp.zeros((ts, LANES), F32)
        for kc in range(nc):
            p = jnp.exp2(_bdot_nt(qt, k_ref[kc * tk:(kc + 1) * tk, :]) - u)
            p_ref[i, t, kc] = p.astype(BF16)
            l_acc = fold(l_acc, p, jnp.add)
        pc = jnp.exp2(_bdot_nt(qt, ck_ref[...]) - u)
        pc_ref[i, t] = pc.astype(BF16)
        l_acc = fold(l_acc, pc, jnp.add)
        return jnp.sum(l_acc, axis=-1, keepdims=True)

    ls = {}
    for i in range(nsub + 1):
        if i < nsub:
            ls[i, 0] = scores_exps(i, 0)
        if i > 0:
            attend(i - 1, ls[i - 1, 0], ls[i - 1, 1])
        if i < nsub:
            ls[i, 1] = scores_exps(i, 1)
    low = functools.reduce(jnp.minimum, ls.values())
    row_sums_ok = jnp.min(low) >= DA_MIN_L

    @pl.when(jnp.logical_not(row_sums_ok))
    def _():
        exact_step()


def _da(dq, dk, dv, lam_p, subln, layer, ctx_kv):
    b, n, _ = dq.shape
    lam_init = 0.8 - 0.6 * math.exp(-0.3 * layer)
    tq = min(DA_TQ, n)
    ts = min(DA_TS, tq)
    tk = min(DA_TK, n)
    has_ctx = ctx_kv is not None
    qmap = lambda bi, h, qi: (bi, qi, h)
    kvmap = lambda bi, h, qi: (bi, 0, h)
    in_specs = [pl.BlockSpec((4, DA_HD), lambda bi, h, qi: (0, 0)),
                pl.BlockSpec((1, 2 * DA_HD), lambda bi, h, qi: (0, 0)),
                pl.BlockSpec((None, tq, LANES), qmap),
                pl.BlockSpec((None, n, LANES), kvmap),
                pl.BlockSpec((None, n, LANES), kvmap)]
    args = [lam_p, subln.reshape(1, 2 * DA_HD), dq, dk, dv]
    scratch = [pltpu.VMEM((tq // ts, 2, n // tk, ts, tk), F32), pltpu.VMEM((tq // ts, 2, n // tk, ts, tk), BF16)]
    if has_ctx:
        ck, cv, kn = ctx_kv
        past = ck.shape[3]
        cmap = lambda bi, h, qi: (bi, layer, h, 0, 0)
        in_specs += [pl.BlockSpec((None, None, None, past, LANES), cmap)] * 2
        in_specs += [pl.BlockSpec((None, n, LANES), lambda bi, h, qi: (bi, 0, 0))]
        args += [ck, cv, kn]
        scratch += [pltpu.VMEM((tq // ts, 2, ts, past), F32), pltpu.VMEM((tq // ts, 2, ts, past), BF16),
                    pltpu.VMEM((2, SUBLANES, LANES), F32)]
    return pl.pallas_call(
        functools.partial(_da_kernel, lam_init=lam_init, has_ctx=has_ctx, tk=tk),
        grid=(b, DA_H, n // tq),
        in_specs=in_specs,
        out_specs=pl.BlockSpec((None, tq, LANES), qmap),
        out_shape=jax.ShapeDtypeStruct((b, n, DA_W), F32),
        scratch_shapes=scratch,
        compiler_params=_cparams(("arbitrary", "arbitrary", "arbitrary")),
        name="da_lat" if has_ctx else "da_ctx",
    )(*args)


def _pair_attend(q, sources):
    lane = lax.broadcasted_iota(jnp.int32, (1, LANES), 1)
    outs = []
    for j in (0, 1):
        qj = jnp.where((lane >= NA_HD) == bool(j), q, 0.0).astype(BF16)
        ss = []
        for k, _, bias in sources:
            s = _bdot_nt(qj, k)
            if bias is not None:
                s = s + bias[j]
            ss.append(s)
        m = ss[0].max(axis=-1, keepdims=True)
        for s in ss[1:]:
            m = jnp.maximum(m, s.max(axis=-1, keepdims=True))
        l = 0.0
        o = 0.0
        for s, (_, v, _) in zip(ss, sources):
            p = jnp.exp2(s - m)
            l = l + p.sum(axis=-1, keepdims=True)
            o = o + _bdot(p, v)
        outs.append(o * (1.0 / l))
    return jnp.where(lane >= NA_HD, outs[1], outs[0])


def _na_lat_kernel(q_ref, k_ref, v_ref, ck_ref, cv_ref, *rest, n_rows):
    b_refs, o_ref = rest[:NA_SUBS], rest[NA_SUBS]
    step = pl.program_id(2)
    ck, cv = ck_ref[...], cv_ref[...]
    for sub in range(NA_SUBS):
        blk = step * NA_SUBS + sub
        row0 = jnp.clip(blk * NA_QROWS - WIN_R // 2, 0, n_rows - NA_KROWS)
        start = pl.multiple_of(row0 * GRID_W, GRID_W)
        kw = k_ref[pl.ds(start, NA_KW), :]
        vw = v_ref[pl.ds(start, NA_KW), :]
        rows = slice(sub * NA_QB, (sub + 1) * NA_QB)
        q = q_ref[rows, :].astype(F32) * NA_QSCALE
        o_ref[rows, :] = _pair_attend(q, [(kw, vw, b_refs[sub]), (ck, cv, None)])


def _na_ctx_kernel(q_ref, k_ref, v_ref, o_ref):
    q = q_ref[...].astype(F32) * NA_QSCALE
    o_ref[...] = _pair_attend(q, [(k_ref[...], v_ref[...], None)])


def _na_bias_tables(rpb, n_rows):
    nrb = n_rows // NA_QROWS
    nh = rpb.shape[0]
    pad = GRID_W - WIN_C
    rpad = jnp.pad(rpb.astype(F32), ((0, 0), (0, 0), (pad, pad)))
    toep = jnp.stack([rpad[:, :, GRID_W - 1 - qc:2 * GRID_W - 1 - qc] for qc in range(GRID_W)], axis=2)
    cols = np.arange(GRID_W)
    s_c = np.clip(cols - WIN_C // 2, 0, GRID_W - WIN_C)[:, None]
    ok_c = (cols[None, :] >= s_c) & (cols[None, :] < s_c + WIN_C)
    toep = jnp.where(jnp.asarray(ok_c), toep, -jnp.inf)
    masked = jnp.full((nh, GRID_W, GRID_W), -jnp.inf, F32)
    tabs = []
    for rb in (0, 1, nrb - 1):
        row0 = min(max(rb * NA_QROWS - WIN_R // 2, 0), n_rows - NA_KROWS)
        q_rows = rb * NA_QROWS + np.arange(NA_QROWS)
        k_rows = row0 + np.arange(NA_KROWS)
        s_r = np.clip(q_rows - WIN_R // 2, 0, n_rows - WIN_R)[:, None]
        ok_r = (k_rows[None, :] >= s_r) & (k_rows[None, :] < s_r + WIN_R)
        off_r = np.clip(k_rows[None, :] - q_rows[:, None] + WIN_R - 1, 0, 2 * WIN_R - 2)
        rows = [jnp.concatenate([toep[:, off_r[qr, kr]] if ok_r[qr, kr] else masked for kr in range(NA_KROWS)],
                                axis=-1) for qr in range(NA_QROWS)]
        tabs.append(jnp.concatenate(rows, axis=-2) * LOG2E)
    return jnp.stack(tabs, axis=1)


def _na_lat(nq, nk, nv, ck, cv, bias, layer):
    b, n, _ = nq.shape
    n_rows = n // GRID_W
    assert n_rows >= 2 * NA_KROWS and n_rows % (NA_QROWS * NA_SUBS) == 0 and NA_QROWS == WIN_R // 2
    nrb = n_rows // NA_QROWS
    past = ck.shape[3]
    kvmap = lambda bi, hp, st: (bi, 0, hp)
    cmap = lambda bi, hp, st: (bi, layer, hp, 0, 0)
    qmap = lambda bi, hp, st: (bi, st, hp)

    def variant(sub):
        def index(bi, hp, st):
            blk = st * NA_SUBS + sub
            return (layer, hp, jnp.where(blk == 0, 0, jnp.where(blk == nrb - 1, 2, 1)), 0, 0)
        return index

    return pl.pallas_call(
        functools.partial(_na_lat_kernel, n_rows=n_rows),
        grid=(b, NA_H // 2, nrb // NA_SUBS),
        in_specs=[pl.BlockSpec((None, NA_SUBS * NA_QB, LANES), qmap),
                  pl.BlockSpec((None, n, LANES), kvmap),
                  pl.BlockSpec((None, n, LANES), kvmap),
                  pl.BlockSpec((None, None, None, past, LANES), cmap),
                  pl.BlockSpec((None, None, None, past, LANES), cmap)]
        + [pl.BlockSpec((None, 2, None, NA_QB, NA_KW), variant(sub)) for sub in range(NA_SUBS)],
        out_specs=pl.BlockSpec((None, NA_SUBS * NA_QB, LANES), qmap),
        out_shape=jax.ShapeDtypeStruct((b, n, NA_W), F32),
        compiler_params=_cparams(("arbitrary", "arbitrary", "arbitrary")),
        name="na_lat",
    )(nq, nk, nv, ck, cv, *([bias] * NA_SUBS))


def _na_ctx(nq, nk, nv):
    b, n, _ = nq.shape
    spec = pl.BlockSpec((None, n, LANES), lambda bi, hp: (bi, 0, hp))
    return pl.pallas_call(
        _na_ctx_kernel,
        grid=(b, NA_H // 2),
        in_specs=[spec, spec, spec],
        out_specs=spec,
        out_shape=jax.ShapeDtypeStruct((b, n, NA_W), F32),
        compiler_params=_cparams(("arbitrary", "arbitrary")),
        name="na_ctx",
    )(nq, nk, nv)


def _split3(x):
    hi = x.astype(BF16)
    r = x - hi.astype(F32)
    mid = r.astype(BF16)
    return hi, mid, (r - mid.astype(F32)).astype(BF16)


def _dot3_left(x, m01):
    hi, mid, lo = _split3(x)
    d = lambda a: jnp.dot(a, m01, preferred_element_type=F32)
    return (d(lo) + d(mid)) + d(hi)


def _dot3_right(m01, x):
    hi, mid, lo = _split3(x)
    d = lambda a: jnp.dot(m01, a, preferred_element_type=F32)
    return (d(lo) + d(mid)) + d(hi)


def _ssd_block(refs, sb, *, bwd, ns, cps):
    (xm_ref, xp_ref, xn_ref, z_ref, dt_ref, dtt_ref, cw_ref, cb_ref, dtb_ref, dtbt_ref, alogt_ref,
     alogx_ref, dexp_ref, nrm_ref, y_ref, xpad_ref, st_ref, yacc_ref, act_ref) = refs
    t = SSM_CHUNK
    tb = cps * t
    gw = SSM_DI // SSM_G
    hpg = SSM_H // SSM_G
    p_shift = SSM_P.bit_length() - 1
    row0 = pl.multiple_of(sb * tb, tb)

    if not bwd:
        xpad_ref[0:SUBLANES, :] = jnp.where(sb == 0, 0.0, xp_ref[...])
        xpad_ref[SUBLANES:SUBLANES + tb, :] = xm_ref[...]
        xpad_ref[SUBLANES + tb:, :] = jnp.where(sb == ns - 1, 0.0, xn_ref[...])
        conv = cb_ref[...]
        for k in range(CONV_W):
            o = SUBLANES - CONV_W // 2 + k
            conv = conv + cw_ref[k:k + 1, :] * xpad_ref[o:o + tb, :]
        act = _silu(conv)
        act_ref[pl.ds(row0, tb), :] = act
    else:
        act = act_ref[pl.ds(row0, tb), :]
    xs = act[:, :SSM_DI]
    bm = act[:, SSM_DI:SSM_DI + SSM_G * SSM_N]
    cm = act[:, SSM_DI + SSM_G * SSM_N:]

    li = lax.broadcasted_iota(jnp.int32, (t, t), 0)
    si = lax.broadcasted_iota(jnp.int32, (t, t), 1)
    keep = (si >= li) if bwd else (si <= li)
    tri = jnp.where(keep, 1.0, 0.0).astype(BF16)
    trit = jnp.where((li >= si) if bwd else (li <= si), 1.0, 0.0).astype(BF16)
    r_i = lax.broadcasted_iota(jnp.int32, (LANES, SSM_DI), 0)
    c_i = lax.broadcasted_iota(jnp.int32, (LANES, SSM_DI), 1)
    d = int(bwd)
    expand = jnp.where(r_i == d * SSM_H + (c_i >> p_shift), 1.0, 0.0).astype(BF16)
    dtv = _softplus(dt_ref[...] + dtb_ref[...])
    dtx = _dot3_left(dtv, expand)
    a_x = dtx * (-jnp.exp(alogx_ref[d:d + 1, :]))
    xdt = xs * dtx
    at = -jnp.exp(alogt_ref[...]) * _softplus(dtt_ref[...] + dtbt_ref[...])
    lane_g = lax.broadcasted_iota(jnp.int32, (1, gw), 1)

    order = range(cps - 1, -1, -1) if bwd else range(cps)
    y_rows = [None] * cps
    for ci in order:
        rows = slice(ci * t, (ci + 1) * t)
        acum = _dot3_right(tri, a_x[rows])
        tot = acum[0:1, :] if bwd else acum[t - 1:t, :]
        e_in = jnp.exp(acum)
        xdt_c = xdt[rows]
        xdt_b = xdt_c.astype(BF16)
        xout_b = (xdt_c * jnp.exp(tot - acum)).astype(BF16)
        acum_t = _dot3_left(at[:, rows], trit)
        y_parts = []
        for g in range(SSM_G):
            cg = cm[rows, g * SSM_N:(g + 1) * SSM_N].astype(BF16)
            bg = bm[rows, g * SSM_N:(g + 1) * SSM_N]
            gmat = _bdot_nt(cg, bg)
            st_g = st_ref[:, g * gw:(g + 1) * gw]
            y_g = _bdot(cg, st_g) * e_in[:, g * gw:(g + 1) * gw]
            xdt_g = xdt_b[:, g * gw:(g + 1) * gw]
            lhs, rhs = [], []
            for hh in range(hpg):
                h = g * hpg + hh
                col = acum[:, h * SSM_P:h * SSM_P + 1]
                row = acum_t[d * SSM_H + h:d * SSM_H + h + 1, :]
                lmat = jnp.exp(jnp.where(keep, col - row, -jnp.inf))
                lhs.append((gmat * lmat).astype(BF16))
                rhs.append(jnp.where((lane_g >> p_shift) == hh, xdt_g, jnp.zeros_like(xdt_g)))
            y_g = y_g + jnp.dot(jnp.concatenate(lhs, axis=1), jnp.concatenate(rhs, axis=0),
                                preferred_element_type=F32)
            y_parts.append(y_g)
            st_ref[:, g * gw:(g + 1) * gw] = (st_g * jnp.exp(tot[:, g * gw:(g + 1) * gw])
                                              + _bdot(bg.T, xout_b[:, g * gw:(g + 1) * gw]))
        y_rows[ci] = jnp.concatenate(y_parts, axis=-1)
    y_dir = jnp.concatenate(y_rows, axis=0)

    if not bwd:
        yacc_ref[pl.ds(row0, tb), :] = y_dir + xs * dexp_ref[...]
    else:
        y = (yacc_ref[pl.ds(row0, tb), :] + y_dir) * _silu(z_ref[...])
        parts = [_rms(y[:, g * gw:(g + 1) * gw], nrm_ref[:, g * gw:(g + 1) * gw]) for g in range(SSM_G)]
        y_ref[...] = jnp.concatenate(parts, axis=-1)


def _ssd_kernel(*refs, ns, cps, has_h0):
    n_in = 14
    if has_h0:
        h0f_ref, h0b_ref = refs[n_in:n_in + 2]
        refs = refs[:n_in] + refs[n_in + 2:]
    hf_ref, hb_ref = refs[n_in + 1:n_in + 3]
    block_refs = refs[:n_in + 1] + refs[n_in + 3:]
    st_ref = block_refs[n_in + 2]
    j = pl.program_id(1)

    @pl.when(j == 0)
    def _():
        st_ref[...] = h0f_ref[...] if has_h0 else jnp.zeros_like(st_ref)

    @pl.when(j == ns)
    def _():
        st_ref[...] = h0b_ref[...] if has_h0 else jnp.zeros_like(st_ref)

    @pl.when(j < ns)
    def _():
        _ssd_block(block_refs, j, bwd=False, ns=ns, cps=cps)

    @pl.when(j >= ns)
    def _():
        _ssd_block(block_refs, 2 * ns - 1 - j, bwd=True, ns=ns, cps=cps)

    @pl.when(j == ns - 1)
    def _():
        hf_ref[...] = st_ref[...]

    @pl.when(j == 2 * ns - 1)
    def _():
        hb_ref[...] = st_ref[...]


def _ssd(sz, sxbc, sdt, p, h0):
    b, n, _ = sz.shape
    t = SSM_CHUNK
    cps = min(SSD_CPS, n // t)
    tb = cps * t
    assert n % tb == 0
    ns = n // tb
    nb8 = n // SUBLANES
    tb8 = tb // SUBLANES
    has_h0 = h0 is not None
    sdt_t = jnp.swapaxes(sdt[:, :, :2 * SSM_H], 1, 2)
    sidx = lambda j: jnp.where(j >= ns, 2 * ns - 1 - j, j)
    fidx = lambda j: jnp.minimum(j, ns - 1)
    zidx = lambda j: jnp.where(j >= ns, 2 * ns - 1 - j, ns - 1)
    const2 = lambda bi, j: (0, 0)
    cw = jnp.zeros((SUBLANES, CONV_CH), F32).at[:CONV_W].set(p["ssm_conv_w"])
    dtb = jnp.zeros((1, DT_PAD), F32).at[0, :2 * SSM_H].set(p["ssm_dt_bias"].reshape(-1))
    dtbt = jnp.broadcast_to(p["ssm_dt_bias"].reshape(2 * SSM_H, 1), (2 * SSM_H, tb))
    alogt = jnp.broadcast_to(p["ssm_a_log"].reshape(2 * SSM_H, 1), (2 * SSM_H, tb))
    alogx = jnp.repeat(p["ssm_a_log"], SSM_P, axis=-1)
    dexp = jnp.repeat(p["ssm_d"], SSM_P).reshape(1, SSM_DI)
    in_specs = [pl.BlockSpec((None, tb, CONV_CH), lambda bi, j: (bi, fidx(j), 0)),
                pl.BlockSpec((None, SUBLANES, CONV_CH), lambda bi, j: (bi, jnp.maximum(fidx(j) * tb8 - 1, 0), 0)),
                pl.BlockSpec((None, SUBLANES, CONV_CH),
                             lambda bi, j: (bi, jnp.minimum((fidx(j) + 1) * tb8, nb8 - 1), 0)),
                pl.BlockSpec((None, tb, SSM_DI), lambda bi, j: (bi, zidx(j), 0)),
                pl.BlockSpec((None, tb, DT_PAD), lambda bi, j: (bi, sidx(j), 0)),
                pl.BlockSpec((None, 2 * SSM_H, tb), lambda bi, j: (bi, 0, sidx(j))),
                pl.BlockSpec((SUBLANES, CONV_CH), const2),
                pl.BlockSpec((1, CONV_CH), const2),
                pl.BlockSpec((1, DT_PAD), const2),
                pl.BlockSpec((2 * SSM_H, tb), const2),
                pl.BlockSpec((2 * SSM_H, tb), const2),
                pl.BlockSpec((2, SSM_DI), const2),
                pl.BlockSpec((1, SSM_DI), const2),
                pl.BlockSpec((1, SSM_DI), const2)]
    args = [sxbc, sxbc, sxbc, sz, sdt, sdt_t, cw, p["ssm_conv_b"].reshape(1, CONV_CH), dtb, dtbt,
            alogt, alogx, dexp, p["ssm_norm"].reshape(1, SSM_DI)]
    st_spec = pl.BlockSpec((None, SSM_N, SSM_DI), lambda bi, j: (bi, 0, 0))
    if has_h0:
        in_specs += [st_spec, st_spec]
        args += list(h0)
    y, hf, hb = pl.pallas_call(
        functools.partial(_ssd_kernel, ns=ns, cps=cps, has_h0=has_h0),
        grid=(b, 2 * ns),
        in_specs=in_specs,
        out_specs=[pl.BlockSpec((None, tb, SSM_DI), lambda bi, j: (bi, zidx(j), 0)), st_spec, st_spec],
        out_shape=[jax.ShapeDtypeStruct((b, n, SSM_DI), F32),
                   jax.ShapeDtypeStruct((b, SSM_N, SSM_DI), F32),
                   jax.ShapeDtypeStruct((b, SSM_N, SSM_DI), F32)],
        scratch_shapes=[pltpu.VMEM((tb + 2 * SUBLANES, CONV_CH), F32),
                        pltpu.VMEM((SSM_N, SSM_DI), F32),
                        pltpu.VMEM((n, SSM_DI), F32),
                        pltpu.VMEM((n, CONV_CH), F32)],
        compiler_params=_cparams(("arbitrary", "arbitrary")),
        name="ssd_lat" if has_h0 else "ssd_ctx",
    )(*args)
    return y, hf, hb


def _post_kernel(x_ref, oda_ref, ossm_ref, ona_ref, gate_ref, mod_ref, wda_ref, wssm_ref, wna_ref, wout_ref,
                 gpost_ref, gpre_ref, wr_ref, br_ref, x1_ref, h2_ref, rg_ref):
    d = x_ref.shape[1]
    tm = x_ref.shape[0]
    ts = min(POST_TS, tm)
    for i in range(tm // ts):
        r = slice(i * ts, (i + 1) * ts)
        mix = (gate_ref[r, 0:d] * _bdot(oda_ref[r, :], wda_ref[...])
               + gate_ref[r, d:2 * d] * _bdot(ossm_ref[r, :], wssm_ref[...])
               + gate_ref[r, 2 * d:3 * d] * _bdot(ona_ref[r, :], wna_ref[...]))
        mix = _bdot(mix, wout_ref[...])
        x1 = x_ref[r, :] + mod_ref[2:3, :] * _rms(mix, gpost_ref[...])
        x1_ref[r, :] = x1
        h2 = _rms(x1, gpre_ref[...]) * (1.0 + mod_ref[4:5, :]) + mod_ref[3:4, :]
        h_hi = h2.astype(BF16)
        h2_ref[r, :] = h_hi

        h_mid = (h2 - h_hi.astype(F32)).astype(BF16)
        a = jnp.dot(h_hi, wr_ref[...], preferred_element_type=F32)
        b = jnp.dot(h_mid, wr_ref[:, :LANES], preferred_element_type=F32)
        s = _sigmoid(a[:, :LANES] + (a[:, LANES:] + b))
        work = s + br_ref[...]
        col = lax.broadcasted_iota(jnp.int32, (ts, LANES), 1).astype(F32)
        picked = jnp.zeros((ts, LANES), F32)
        for _ in range(TOP_K):
            m = jnp.max(work, axis=-1, keepdims=True)
            first = jnp.min(jnp.where(work == m, col, float(LANES)), axis=-1, keepdims=True)
            hit = col == first
            picked = jnp.where(hit, 1.0, picked)
            work = jnp.where(hit, -jnp.inf, work)
        sel = picked * s
        g = sel / jnp.sum(sel, axis=-1, keepdims=True) * ROUTE_SCALE
        g_hi = g.astype(BF16).astype(F32)
        r1 = g - g_hi
        g_mid = r1.astype(BF16).astype(F32)
        g_lo = r1 - g_mid
        packed = g_hi + pltpu.roll(g_mid, N_EXPERTS, 1) + pltpu.roll(g_lo, 2 * N_EXPERTS, 1)
        rg_ref[r, :] = packed.astype(rg_ref.dtype)


def _post(x3, o_da, o_ssm, o_na, gates, mods, p):
    b, n, d = x3.shape
    tm = min(POST_TM, n)
    tps = n // tm
    t = b * n
    per_batch_mods = mods.shape[0] > 1
    mod_idx = (lambda i: (i // tps, 0, 0)) if per_batch_mods else (lambda i: (0, 0, 0))
    row = lambda i: (i, 0)
    const2 = lambda i: (0, 0)
    flat = lambda a: a.reshape(t, a.shape[-1])
    wspec = lambda k: pl.BlockSpec((k, d), const2)
    w_r = jnp.zeros((d, LANES), F32).at[:, :N_EXPERTS].set(p["w_router"])
    w_r_hi = w_r.astype(BF16)
    w_router = jnp.concatenate([w_r_hi, (w_r - w_r_hi.astype(F32)).astype(BF16)], axis=1)
    b_router = jnp.full((1, LANES), -jnp.inf, F32).at[0, :N_EXPERTS].set(p["b_router"])
    x1, h2, route = pl.pallas_call(
        _post_kernel,
        grid=(t // tm,),
        in_specs=[pl.BlockSpec((tm, d), row), pl.BlockSpec((tm, DA_W), row), pl.BlockSpec((tm, SSM_DI), row),
                  pl.BlockSpec((tm, NA_W), row), pl.BlockSpec((tm, 3 * d), row),
                  pl.BlockSpec((None, 6, d), mod_idx),
                  wspec(DA_W), wspec(SSM_DI), wspec(NA_W), wspec(d),
                  pl.BlockSpec((1, d), const2), pl.BlockSpec((1, d), const2),
                  pl.BlockSpec((d, 2 * LANES), const2), pl.BlockSpec((1, LANES), const2)],
        out_specs=[pl.BlockSpec((tm, d), row), pl.BlockSpec((tm, d), row), pl.BlockSpec((tm, LANES), row)],
        out_shape=[jax.ShapeDtypeStruct((t, d), F32), jax.ShapeDtypeStruct((t, d), BF16),
                   jax.ShapeDtypeStruct((t, LANES), BF16)],
        compiler_params=_cparams(("arbitrary",)),
        name="post",
    )(flat(x3), flat(o_da), flat(o_ssm), flat(o_na), flat(gates), mods,
      p["w_br_da"], p["w_br_ssm"], p["w_br_na"], p["w_out"],
      p["g_post1"].reshape(1, d), p["g_pre2"].reshape(1, d), w_router, b_router)
    return x1, h2, route


def _moe_kernel(x1_ref, h2_ref, rg_ref, mod_ref, wsg_ref, wsu_ref, wsd_ref, weg_ref, weu_ref, wed_ref,
                gpost_ref, o_ref, acc_ref):
    e = pl.program_id(1)
    hb = h2_ref[...]

    @pl.when(e == 0)
    def _():
        acc_ref[...] = _bdot(_silu(_bdot(hb, wsg_ref[...])) * _bdot(hb, wsu_ref[...]), wsd_ref[...])

    width = MOE_EG * EXPERT_FF
    r_i = lax.broadcasted_iota(jnp.int32, (LANES, width), 0)
    c_i = lax.broadcasted_iota(jnp.int32, (LANES, width), 1)
    ff_shift = EXPERT_FF.bit_length() - 1
    onehot = jnp.where((r_i < 3 * N_EXPERTS) & ((r_i & (N_EXPERTS - 1)) == e * MOE_EG + (c_i >> ff_shift)),
                       1.0, 0.0).astype(BF16)
    g_exp = jnp.dot(rg_ref[...], onehot, preferred_element_type=F32)
    act = _silu(_bdot(hb, weg_ref[...])) * _bdot(hb, weu_ref[...]) * g_exp
    acc_ref[...] += _bdot(act, wed_ref[...])

    @pl.when(e == pl.num_programs(1) - 1)
    def _():
        o_ref[...] = x1_ref[...] + mod_ref[5:6, :] * _rms(acc_ref[...], gpost_ref[...])


def _moe(x1, h2, route, mods, p, n):
    t, d = x1.shape
    tm = MOE_TM
    per_batch_mods = mods.shape[0] > 1
    if per_batch_mods:
        assert n % tm == 0
    tps = max(n // tm, 1)
    mod_idx = (lambda i, e: (i // tps, 0, 0)) if per_batch_mods else (lambda i, e: (0, 0, 0))
    row = lambda i, e: (i, 0)
    const2 = lambda i, e: (0, 0)
    width = MOE_EG * EXPERT_FF
    resident = dict(pipeline_mode=pl.Buffered(1)) if MOE_EG == N_EXPERTS else {}
    return pl.pallas_call(
        _moe_kernel,
        grid=(t // tm, N_EXPERTS // MOE_EG),
        in_specs=[pl.BlockSpec((tm, d), row), pl.BlockSpec((tm, d), row), pl.BlockSpec((tm, LANES), row),
                  pl.BlockSpec((None, 6, d), mod_idx),
                  pl.BlockSpec((d, SHARED_FF), const2), pl.BlockSpec((d, SHARED_FF), const2),
                  pl.BlockSpec((SHARED_FF, d), const2),
                  pl.BlockSpec((d, width), lambda i, e: (0, e), **resident),
                  pl.BlockSpec((d, width), lambda i, e: (0, e), **resident),
                  pl.BlockSpec((width, d), lambda i, e: (e, 0), **resident),
                  pl.BlockSpec((1, d), const2)],
        out_specs=pl.BlockSpec((tm, d), row),
        out_shape=jax.ShapeDtypeStruct((t, d), F32),
        scratch_shapes=[pltpu.VMEM((tm, d), F32)],
        compiler_params=_cparams(("arbitrary", "arbitrary")),
        name="moe",
    )(x1, h2, route, mods,
      p["w_s_gate"], p["w_s_up"], p["w_s_down"], p["w_e_gate"], p["w_e_up"], p["w_e_down"],
      p["g_post2"].reshape(1, d))


def _rope_tables(n):
    pos = jnp.arange(n)
    rows = (pos // GRID_W).astype(F32)
    cols = (pos % GRID_W).astype(F32)
    n_freq = DA_HD // 4
    inv = ROPE_BASE ** (-jnp.arange(n_freq, dtype=F32) / n_freq)
    ar, ac = rows[:, None] * inv, cols[:, None] * inv
    zero = jnp.zeros_like(ar)

    def lanes(first_r, second_r, first_c, second_c):
        half = jnp.concatenate([first_r, second_r, first_c, second_c], axis=-1)
        return jnp.concatenate([half, half], axis=-1)

    cos = lanes(jnp.cos(ar), jnp.cos(ar), jnp.cos(ac), jnp.cos(ac))
    s_first = lanes(-jnp.sin(ar), zero, -jnp.sin(ac), zero)
    s_second = lanes(zero, jnp.sin(ar), zero, jnp.sin(ac))
    return cos, s_first, s_second


def _prep_layer_params(stacked, layer):
    p = {name: arr[layer] for name, arr in stacked.items()}
    w_in = p["w_in"]
    dt0 = 3 * DA_W + SSM_DI + CONV_CH
    d = w_in.shape[0]
    p["w_in_p"] = jnp.concatenate(
        [w_in[:, :dt0], w_in[:, dt0 + 2 * SSM_H:], w_in[:, dt0:dt0 + 2 * SSM_H],
         jnp.zeros((d, DT_PAD - 2 * SSM_H), w_in.dtype)], axis=-1).astype(BF16)
    for name in ("w_gate", "w_br_da", "w_br_ssm", "w_br_na", "w_out", "w_s_gate", "w_s_up", "w_s_down"):
        p[name] = p[name].astype(BF16)
    ff = N_EXPERTS * EXPERT_FF
    p["w_e_gate"] = p["w_e_gate"].astype(BF16).transpose(1, 0, 2).reshape(d, ff)
    p["w_e_up"] = p["w_e_up"].astype(BF16).transpose(1, 0, 2).reshape(d, ff)
    p["w_e_down"] = p["w_e_down"].astype(BF16).reshape(ff, d)
    return p


def _trunk_layer(x3, mods, p, layer, ctx):
    b, n, d = x3.shape
    latent = ctx is not None
    rope_tabs = ctx[6] if latent else None
    qkv_dtype = BF16 if latent else F32
    outs = _pre(x3, mods, p["g_pre1"], p["w_in_p"], p["w_gate"], rope_tabs, qkv_dtype)
    dq, dk, dv, sz, sxbc, nq, nk, nv, sdt, gates = outs[:10]
    if latent:
        ck, cv, cnk, cnv, h0f, h0b, _, na_bias = ctx
        o_da = _da(dq, dk, dv, p["da_lambda"], p["da_subln"], layer, (ck, cv, outs[10]))
        o_na = _na_lat(nq, nk, nv, cnk, cnv, na_bias, layer)
        o_ssm, hf, hb = _ssd(sz, sxbc, sdt, p, (h0f, h0b))
    else:
        o_da = _da(dq, dk, dv, p["da_lambda"], p["da_subln"], layer, None)
        o_na = _na_ctx(nq, nk, nv)
        o_ssm, hf, hb = _ssd(sz, sxbc, sdt, p, None)
    x1, h2, route = _post(x3, o_da, o_ssm, o_na, gates, mods, p)
    x2 = _moe(x1, h2, route, mods, p, n).reshape(b, n, d)
    state = None
    if not latent:
        to_state = lambda s: s.reshape(b, SSM_N, SSM_H, SSM_P).transpose(0, 2, 3, 1)
        state = (dk.reshape(b, n, DA_H, 2, DA_HD).transpose(0, 2, 1, 3, 4),
                 dv.reshape(b, n, DA_H, 2 * DA_HD).transpose(0, 2, 1, 3),
                 nk.reshape(b, n, NA_H, NA_HD).transpose(0, 2, 1, 3),
                 nv.reshape(b, n, NA_H, NA_HD).transpose(0, 2, 1, 3),
                 to_state(hf), to_state(hb))
    return x2, state


def kernel(x_prompt, x_sample, c, cache_da_k, cache_da_v, cache_na_k, cache_na_v, state_ssm_f, state_ssm_b,
           c_ctx, w_ada, b_ada, g_pre1, g_post1, g_pre2, g_post2, w_in, w_gate, da_lambda, da_subln,
           ssm_conv_w, ssm_conv_b, ssm_dt_bias, ssm_a_log, ssm_d, ssm_norm, na_rpb, w_br_da, w_br_ssm,
           w_br_na, w_out, w_router, b_router, w_e_gate, w_e_up, w_e_down, w_s_gate, w_s_up, w_s_down):
    stacked = {
        "g_pre1": g_pre1, "g_post1": g_post1, "g_pre2": g_pre2, "g_post2": g_post2, "w_in": w_in,
        "w_gate": w_gate, "da_lambda": da_lambda, "da_subln": da_subln, "ssm_conv_w": ssm_conv_w,
        "ssm_conv_b": ssm_conv_b, "ssm_dt_bias": ssm_dt_bias, "ssm_a_log": ssm_a_log, "ssm_d": ssm_d,
        "ssm_norm": ssm_norm, "na_rpb": na_rpb, "w_br_da": w_br_da, "w_br_ssm": w_br_ssm, "w_br_na": w_br_na,
        "w_out": w_out, "w_router": w_router, "b_router": b_router, "w_e_gate": w_e_gate, "w_e_up": w_e_up,
        "w_e_down": w_e_down, "w_s_gate": w_s_gate, "w_s_up": w_s_up, "w_s_down": w_s_down,
    }
    depth = w_ada.shape[0]
    nb = x_sample.shape[0]
    d = x_prompt.shape[-1]
    rows = -(-(1 + nb) // SUBLANES) * SUBLANES
    cvec = jnp.zeros((rows, d), F32).at[0].set(c_ctx).at[1:1 + nb].set(c)
    mods = _ada(cvec, w_ada, b_ada).reshape(depth, rows, 6, d)

    past = cache_da_k.shape[3]
    ck_all = cache_da_k.reshape(nb, depth, DA_H, past, 2 * DA_HD)
    pair = lambda a: a.reshape(nb, depth, NA_H // 2, 2, past, NA_HD).transpose(0, 1, 2, 4, 3, 5).reshape(
        nb, depth, NA_H // 2, past, 2 * NA_HD)
    cnk_all, cnv_all = pair(cache_na_k), pair(cache_na_v)
    st_t = lambda s: s.transpose(0, 1, 4, 2, 3).reshape(nb, depth, SSM_N, SSM_DI)
    h0f_all, h0b_all = st_t(state_ssm_f), st_t(state_ssm_b)

    n_lat = x_sample.shape[1]
    rope_tabs = _rope_tables(n_lat)
    na_bias = _na_bias_tables(na_rpb.reshape((depth * NA_H,) + na_rpb.shape[2:]), n_lat // GRID_W)
    na_bias = na_bias.reshape((depth, NA_H) + na_bias.shape[1:])

    xp, xs = x_prompt, x_sample
    new = ([], [], [], [], [], [])
    for layer in range(depth):
        p = _prep_layer_params(stacked, layer)
        xp, st = _trunk_layer(xp, mods[layer, 0:1], p, layer, None)
        for lst, arr in zip(new, st):
            lst.append(arr)
        ctx = (ck_all, cache_da_v, cnk_all, cnv_all, h0f_all[:, layer], h0b_all[:, layer], rope_tabs, na_bias)
        xs, _ = _trunk_layer(xs, mods[layer, 1:1 + nb], p, layer, ctx)
    return (xp, xs) + tuple(jnp.stack(lst, axis=1) for lst in new)
```

```python
import functools
import math

import numpy as np
import jax
import jax.numpy as jnp
from jax import lax
from jax.experimental import pallas as pl
from jax.experimental.pallas import tpu as pltpu

F32 = jnp.float32
BF16 = jnp.bfloat16

D_MODEL = 1024
GRID_W = 64
EPS = 1e-6
DA_H, DA_HD = 4, 64
DA_W = DA_H * 2 * DA_HD
ROPE_BASE = 10000.0
SSM_H, SSM_P, SSM_G, SSM_N = 8, 64, 2, 128
SSM_DI = SSM_H * SSM_P
SSM_CHUNK = 128
CONV_W = 5
CONV_CH = SSM_DI + 2 * SSM_G * SSM_N
NA_H, NA_HD = 8, 64
NA_W = NA_H * NA_HD
WIN_R, WIN_C = 8, 16
N_EXPERTS, TOP_K, EXPERT_FF, SHARED_FF = 32, 4, 128, 256
ROUTE_SCALE = 2.5

LANES = 128
SUBLANES = 8
VMEM_LIMIT_BYTES = 56 * 2**20

PRE_TM = 256
POST_TM = 512
POST_TS = 512
MOE_TM = 256
MOE_EG = 32
DA_TQ = 1024
DA_TS = 256
SSD_CPS = 4
DA_BOUND_SLACK = 1.0 + 2.0 ** -6
DA_MIN_L = 2.0 ** -60
DA_TK = 512
NA_QROWS = WIN_R // 2
NA_QB = NA_QROWS * GRID_W
NA_SUBS = 8
NA_KROWS = NA_QROWS + WIN_R
NA_KW = NA_KROWS * GRID_W
DT_PAD = LANES

OFF_DQ, OFF_DK, OFF_DV, OFF_SZ, OFF_XBC, OFF_NQ, OFF_NK, OFF_NV, OFF_DT = (
    0, 512, 1024, 1536, 2048, 3072, 3584, 4096, 4608)
IN_WP = OFF_DT + DT_PAD

_NT = (((1,), (1,)), ((), ()))
LOG2E = math.log2(math.e)
NA_QSCALE = NA_HD ** -0.5 * LOG2E


def _cparams(sem):
    return pltpu.CompilerParams(dimension_semantics=sem, vmem_limit_bytes=VMEM_LIMIT_BYTES)


def _bdot(a, b):
    return jnp.dot(a.astype(BF16), b.astype(BF16), preferred_element_type=F32)


def _bdot_nt(a, b):
    return lax.dot_general(a.astype(BF16), b.astype(BF16), _NT, preferred_element_type=F32)


def _fdot(a, b):
    return jnp.dot(a, b, preferred_element_type=F32, precision=lax.Precision.HIGHEST)


def _sigmoid(x):
    return 1.0 / (1.0 + jnp.exp(-x))


def _silu(x):
    return x * _sigmoid(x)


def _softplus(x):
    return jnp.maximum(x, 0.0) + jnp.log(1.0 + jnp.exp(-jnp.abs(x)))


def _rms(x, g):
    return x * lax.rsqrt(jnp.mean(x * x, axis=-1, keepdims=True) + EPS) * g


def _ada_kernel(c_ref, w_ref, b_ref, o_ref):
    o_ref[...] = _bdot(_silu(c_ref[...]), w_ref[...]) + b_ref[...]


def _ada(cvec, w_ada, b_ada):
    nl, d, d6 = w_ada.shape
    r = cvec.shape[0]
    tn = 1024
    return pl.pallas_call(
        _ada_kernel,
        grid=(nl, d6 // tn),
        in_specs=[pl.BlockSpec((r, d), lambda l, j: (0, 0)),
                  pl.BlockSpec((None, d, tn), lambda l, j: (l, 0, j)),
                  pl.BlockSpec((None, 1, tn), lambda l, j: (l, 0, j))],
        out_specs=pl.BlockSpec((None, r, tn), lambda l, j: (l, 0, j)),
        out_shape=jax.ShapeDtypeStruct((nl, r, d6), F32),
        compiler_params=_cparams(("arbitrary", "arbitrary")),
        name="ada",
    )(cvec, w_ada, b_ada.reshape(nl, 1, d6))


def _pre_kernel(*refs, rope):
    if rope:
        x_ref, mod_ref, g_ref, win_ref, wg_ref, cos_ref, sa_ref, sb_ref = refs[:8]
        outs = refs[8:]
    else:
        x_ref, mod_ref, g_ref, win_ref, wg_ref = refs[:5]
        outs = refs[5:]
    dq_ref, dk_ref, dv_ref, sz_ref, xbc_ref, nq_ref, nk_ref, nv_ref, dt_ref, gate_ref = outs[:10]
    x = x_ref[...]
    h = _rms(x, g_ref[...]) * (1.0 + mod_ref[1:2, :]) + mod_ref[0:1, :]
    hb = h.astype(BF16)

    def proj(off, width):
        return jnp.dot(hb, win_ref[:, off:off + width], preferred_element_type=F32)

    lane = lax.broadcasted_iota(jnp.int32, (1, LANES), 1)
    for off, ref in ((OFF_DQ, dq_ref), (OFF_DK, dk_ref)):
        if rope:
            cos, sa, sb = cos_ref[...], sa_ref[...], sb_ref[...]
            kn = jnp.zeros((x.shape[0], LANES), F32)
            for hd in range(DA_H):
                u = proj(off + hd * LANES, LANES)
                u = u * cos + pltpu.roll(u, LANES - 16, 1) * sa + pltpu.roll(u, 16, 1) * sb
                ub = u.astype(ref.dtype)
                ref[:, hd * LANES:(hd + 1) * LANES] = ub
                if ref is dk_ref:
                    sq = ub.astype(F32) * ub.astype(F32)
                    for t in (0, 1):
                        nt = jnp.sum(jnp.where((lane >= DA_HD) == bool(t), sq, 0.0), axis=-1, keepdims=True)
                        kn = jnp.where(lane == 2 * hd + t, nt, kn)
            if ref is dk_ref:
                outs[10][...] = kn
        else:
            ref[...] = proj(off, DA_W).astype(ref.dtype)
    dv_ref[...] = proj(OFF_DV, DA_W).astype(dv_ref.dtype)
    sz_ref[...] = proj(OFF_SZ, SSM_DI)
    xbc_ref[...] = proj(OFF_XBC, CONV_CH)
    nq_ref[...] = proj(OFF_NQ, NA_W).astype(nq_ref.dtype)
    nk_ref[...] = proj(OFF_NK, NA_W).astype(nk_ref.dtype)
    nv_ref[...] = proj(OFF_NV, NA_W).astype(nv_ref.dtype)
    dt_ref[...] = proj(OFF_DT, DT_PAD)
    gate_ref[...] = _sigmoid(jnp.dot(hb, wg_ref[...], preferred_element_type=F32))


def _pre(x3, mods, g_pre1, w_in_p, w_gate, rope_tabs, qkv_dtype):
    b, n, d = x3.shape
    tm = PRE_TM
    tps = n // tm
    t = b * n
    x2 = x3.reshape(t, d)
    per_batch_mods = mods.shape[0] > 1
    mod_idx = (lambda i: (i // tps, 0, 0)) if per_batch_mods else (lambda i: (0, 0, 0))
    const2 = lambda i: (0, 0)
    row = lambda i: (i, 0)
    in_specs = [pl.BlockSpec((tm, d), row),
                pl.BlockSpec((None, 6, d), mod_idx),
                pl.BlockSpec((1, d), const2),
                pl.BlockSpec((d, IN_WP), const2, pipeline_mode=pl.Buffered(1)),
                pl.BlockSpec((d, 3 * d), const2, pipeline_mode=pl.Buffered(1))]
    args = [x2, mods, g_pre1.reshape(1, d), w_in_p, w_gate]
    rope = rope_tabs is not None
    if rope:
        in_specs += [pl.BlockSpec((tm, LANES), lambda i: (i % tps, 0))] * 3
        args += list(rope_tabs)
    widths = (DA_W, DA_W, DA_W, SSM_DI, CONV_CH, NA_W, NA_W, NA_W, DT_PAD, 3 * d)
    dtypes = (qkv_dtype, qkv_dtype, qkv_dtype, F32, F32, qkv_dtype, qkv_dtype, qkv_dtype, F32, F32)
    if rope:
        widths += (LANES,)
        dtypes += (F32,)
    outs = pl.pallas_call(
        functools.partial(_pre_kernel, rope=rope),
        grid=(t // tm,),
        in_specs=in_specs,
        out_specs=[pl.BlockSpec((tm, w), row) for w in widths],
        out_shape=[jax.ShapeDtypeStruct((t, w), dt) for w, dt in zip(widths, dtypes)],
        compiler_params=_cparams(("arbitrary",)),
        name="pre_rope" if rope else "pre",
    )(*args)
    return [o.reshape(b, n, o.shape[-1]) for o in outs]


def _da_kernel(*refs, lam_init, has_ctx, tk):
    if has_ctx:
        (lam_ref, sub_ref, q_ref, k_ref, v_ref, ck_ref, cv_ref, kn_ref, o_ref, s_ref, p_ref, sc_ref, pc_ref,
         ku_ref, pf_ref, pcf_ref) = refs
    else:
        lam_ref, sub_ref, q_ref, k_ref, v_ref, o_ref, s_ref, p_ref = refs
    _, _, ts, _ = p_ref.shape
    nsub = q_ref.shape[0] // ts
    nc = k_ref.shape[0] // tk
    lf = lam_ref[...]
    lam = (jnp.exp(jnp.sum(lf[0:1] * lf[1:2], axis=-1, keepdims=True))
           - jnp.exp(jnp.sum(lf[2:3] * lf[3:4], axis=-1, keepdims=True)) + lam_init)
    lane = lax.broadcasted_iota(jnp.int32, (1, LANES), 1)

    def fold(acc, x, op):
        for j in range(x.shape[1] // LANES):
            acc = op(acc, x[:, j * LANES:(j + 1) * LANES])
        return acc

    def scores(i, t):
        q = q_ref[i * ts:(i + 1) * ts, :].astype(F32) * (DA_HD ** -0.5 * LOG2E)
        qt = jnp.where((lane >= DA_HD) == bool(t), q, 0.0).astype(BF16)
        m_acc = jnp.full((ts, LANES), -jnp.inf, F32)
        for kc in range(nc):
            s = _bdot_nt(qt, k_ref[kc * tk:(kc + 1) * tk, :])
            s_ref[t, kc] = s
            m_acc = fold(m_acc, s, jnp.maximum)
        if has_ctx:
            sc = _bdot_nt(qt, ck_ref[...])
            sc_ref[t] = sc
            m_acc = fold(m_acc, sc, jnp.maximum)
        return jnp.max(m_acc, axis=-1, keepdims=True)

    def exps(i, t, m):
        l_acc = jnp.zeros((ts, LANES), F32)
        for kc in range(nc):
            p = jnp.exp2(s_ref[t, kc] - m)
            p_ref[t, kc] = p.astype(BF16)
            l_acc = fold(l_acc, p, jnp.add)
        if has_ctx:
            pc = jnp.exp2(sc_ref[t] - m)
            pc_ref[t] = pc.astype(BF16)
            l_acc = fold(l_acc, pc, jnp.add)
        return jnp.sum(l_acc, axis=-1, keepdims=True)

    def attend(i, l0, l1, p_src, pc_src):
        r = (lam * l0 / l1).astype(BF16)
        acc = jnp.zeros((ts, LANES), F32)
        for kc in range(nc):
            acc = acc + jnp.dot(p_src[0, kc] - p_src[1, kc] * r,
                                v_ref[kc * tk:(kc + 1) * tk, :].astype(BF16), preferred_element_type=F32)
        if has_ctx:
            acc = acc + jnp.dot(pc_src[0] - pc_src[1] * r, cv_ref[...].astype(BF16),
                                preferred_element_type=F32)
        o = _rms(acc * (1.0 / l0), sub_ref[...]) * (1.0 - lam_init)
        o_ref[i * ts:(i + 1) * ts, :] = o.astype(o_ref.dtype)

    def exact_step():
        for i in range(nsub):
            l0 = exps(i, 0, scores(i, 0))
            l1 = exps(i, 1, scores(i, 1))
            attend(i, l0, l1, p_ref, pc_ref if has_ctx else None)

    if not has_ctx:
        exact_step()
        return

    @pl.when(pl.program_id(2) == 0)
    def _():
        hd = pl.program_id(1)
        kmax = jnp.max(kn_ref[...], axis=0, keepdims=True)
        ckf = ck_ref[...].astype(BF16).astype(F32)
        c2 = ckf * ckf
        for t in (0, 1):
            k_lat = jnp.max(jnp.where(lane == 2 * hd + t, kmax, 0.0), axis=-1, keepdims=True)
            k_ctx = jnp.max(jnp.sum(jnp.where((lane >= DA_HD) == bool(t), c2, 0.0), axis=-1, keepdims=True),
                            axis=0, keepdims=True)
            ku_ref[t] = jnp.broadcast_to(jnp.sqrt(jnp.maximum(k_lat, k_ctx)), ku_ref.shape[1:])

    def scores_exps(i, t):
        q = q_ref[i * ts:(i + 1) * ts, :].astype(F32) * (DA_HD ** -0.5 * LOG2E)
        qt = jnp.where((lane >= DA_HD) == bool(t), q, 0.0).astype(BF16)
        qf = qt.astype(F32)
        u = jnp.sqrt(jnp.sum(qf * qf, axis=-1, keepdims=True)) * (ku_ref[t, 0:1, 0:1] * DA_BOUND_SLACK)
        l_acc = jnp.zeros((ts, LANES), F32)
        for kc in range(nc):
            p = jnp.exp2(_bdot_nt(qt, k_ref[kc * tk:(kc + 1) * tk, :]) - u)
            pf_ref[i, t, kc] = p.astype(BF16)
            l_acc = fold(l_acc, p, jnp.add)
        pc = jnp.exp2(_bdot_nt(qt, ck_ref[...]) - u)
        pcf_ref[i, t] = pc.astype(BF16)
        l_acc = fold(l_acc, pc, jnp.add)
        return jnp.sum(l_acc, axis=-1, keepdims=True)

    ls = {}
    for i in range(nsub + 1):
        if i < nsub:
            ls[i, 0] = scores_exps(i, 0)
        if i > 0:
            attend(i - 1, ls[i - 1, 0], ls[i - 1, 1], pf_ref.at[i - 1], pcf_ref.at[i - 1])
        if i < nsub:
            ls[i, 1] = scores_exps(i, 1)
    low = functools.reduce(jnp.minimum, ls.values())
    row_sums_ok = jnp.min(low) >= DA_MIN_L

    @pl.when(jnp.logical_not(row_sums_ok))
    def _():
        exact_step()


def _da(dq, dk, dv, lam_p, subln, layer, ctx_kv):
    b, n, _ = dq.shape
    lam_init = 0.8 - 0.6 * math.exp(-0.3 * layer)
    tq = min(DA_TQ, n)
    ts = min(DA_TS, tq)
    tk = min(DA_TK, n)
    has_ctx = ctx_kv is not None
    qmap = lambda bi, h, qi: (bi, qi, h)
    kvmap = lambda bi, h, qi: (bi, 0, h)
    in_specs = [pl.BlockSpec((4, DA_HD), lambda bi, h, qi: (0, 0)),
                pl.BlockSpec((1, 2 * DA_HD), lambda bi, h, qi: (0, 0)),
                pl.BlockSpec((None, tq, LANES), qmap),
                pl.BlockSpec((None, n, LANES), kvmap),
                pl.BlockSpec((None, n, LANES), kvmap)]
    args = [lam_p, subln.reshape(1, 2 * DA_HD), dq, dk, dv]
    scratch = [pltpu.VMEM((2, n // tk, ts, tk), F32), pltpu.VMEM((2, n // tk, ts, tk), BF16)]
    if has_ctx:
        ck, cv, kn = ctx_kv
        past = ck.shape[3]
        cmap = lambda bi, h, qi: (bi, layer, h, 0, 0)
        in_specs += [pl.BlockSpec((None, None, None, past, LANES), cmap)] * 2
        in_specs += [pl.BlockSpec((None, n, LANES), lambda bi, h, qi: (bi, 0, 0))]
        args += [ck, cv, kn]
        scratch += [pltpu.VMEM((2, ts, past), F32), pltpu.VMEM((2, ts, past), BF16),
                    pltpu.VMEM((2, SUBLANES, LANES), F32),
                    pltpu.VMEM((tq // ts, 2, n // tk, ts, tk), BF16), pltpu.VMEM((tq // ts, 2, ts, past), BF16)]
    return pl.pallas_call(
        functools.partial(_da_kernel, lam_init=lam_init, has_ctx=has_ctx, tk=tk),
        grid=(b, DA_H, n // tq),
        in_specs=in_specs,
        out_specs=pl.BlockSpec((None, tq, LANES), qmap),
        out_shape=jax.ShapeDtypeStruct((b, n, DA_W), F32),
        scratch_shapes=scratch,
        compiler_params=_cparams(("arbitrary", "arbitrary", "arbitrary")),
        name="da_lat" if has_ctx else "da_ctx",
    )(*args)


def _pair_attend(q, sources):
    lane = lax.broadcasted_iota(jnp.int32, (1, LANES), 1)
    outs = []
    for j in (0, 1):
        qj = jnp.where((lane >= NA_HD) == bool(j), q, 0.0).astype(BF16)
        ss = []
        for k, _, bias in sources:
            s = _bdot_nt(qj, k)
            if bias is not None:
                s = s + bias(j)
            ss.append(s)
        m = ss[0].max(axis=-1, keepdims=True)
        for s in ss[1:]:
            m = jnp.maximum(m, s.max(axis=-1, keepdims=True))
        l = 0.0
        o = 0.0
        for s, (_, v, _) in zip(ss, sources):
            p = jnp.exp2(s - m)
            l = l + p.sum(axis=-1, keepdims=True)
            o = o + _bdot(p, v)
        outs.append(o * (1.0 / l))
    return jnp.where(lane >= NA_HD, outs[1], outs[0])


def _na_lat_kernel(q_ref, k_ref, v_ref, ck_ref, cv_ref, b_ref, o_ref, *, n_rows):
    step = pl.program_id(2)
    n_blocks = n_rows // NA_QROWS
    ck, cv = ck_ref[...], cv_ref[...]
    for sub in range(NA_SUBS):
        blk = step * NA_SUBS + sub
        var = jnp.where(blk == 0, 0, jnp.where(blk == n_blocks - 1, 2, 1))
        row0 = jnp.clip(blk * NA_QROWS - WIN_R // 2, 0, n_rows - NA_KROWS)
        start = pl.multiple_of(row0 * GRID_W, GRID_W)
        kw = k_ref[pl.ds(start, NA_KW), :]
        vw = v_ref[pl.ds(start, NA_KW), :]
        rows = slice(sub * NA_QB, (sub + 1) * NA_QB)
        q = q_ref[rows, :].astype(F32) * NA_QSCALE
        bias = functools.partial(lambda j, var: b_ref[j, var], var=var)
        o_ref[rows, :] = _pair_attend(q, [(kw, vw, bias), (ck, cv, None)])


def _na_ctx_kernel(q_ref, k_ref, v_ref, o_ref):
    q = q_ref[...].astype(F32) * NA_QSCALE
    o_ref[...] = _pair_attend(q, [(k_ref[...], v_ref[...], None)])


def _na_bias_tables(rpb, n_rows):
    nrb = n_rows // NA_QROWS
    nh = rpb.shape[0]
    pad = GRID_W - WIN_C
    rpad = jnp.pad(rpb.astype(F32), ((0, 0), (0, 0), (pad, pad)))
    toep = jnp.stack([rpad[:, :, GRID_W - 1 - qc:2 * GRID_W - 1 - qc] for qc in range(GRID_W)], axis=2)
    cols = np.arange(GRID_W)
    s_c = np.clip(cols - WIN_C // 2, 0, GRID_W - WIN_C)[:, None]
    ok_c = (cols[None, :] >= s_c) & (cols[None, :] < s_c + WIN_C)
    toep = jnp.where(jnp.asarray(ok_c), toep, -jnp.inf)
    masked = jnp.full((nh, GRID_W, GRID_W), -jnp.inf, F32)
    tabs = []
    for rb in (0, 1, nrb - 1):
        row0 = min(max(rb * NA_QROWS - WIN_R // 2, 0), n_rows - NA_KROWS)
        q_rows = rb * NA_QROWS + np.arange(NA_QROWS)
        k_rows = row0 + np.arange(NA_KROWS)
        s_r = np.clip(q_rows - WIN_R // 2, 0, n_rows - WIN_R)[:, None]
        ok_r = (k_rows[None, :] >= s_r) & (k_rows[None, :] < s_r + WIN_R)
        off_r = np.clip(k_rows[None, :] - q_rows[:, None] + WIN_R - 1, 0, 2 * WIN_R - 2)
        rows = [jnp.concatenate([toep[:, off_r[qr, kr]] if ok_r[qr, kr] else masked for kr in range(NA_KROWS)],
                                axis=-1) for qr in range(NA_QROWS)]
        tabs.append(jnp.concatenate(rows, axis=-2) * LOG2E)
    return jnp.stack(tabs, axis=1)


def _na_lat(nq, nk, nv, ck, cv, bias, layer):
    b, n, _ = nq.shape
    n_rows = n // GRID_W
    assert n_rows >= 2 * NA_KROWS and n_rows % (NA_QROWS * NA_SUBS) == 0 and NA_QROWS == WIN_R // 2
    nrb = n_rows // NA_QROWS
    past = ck.shape[3]
    kvmap = lambda bi, hp, st: (bi, 0, hp)
    cmap = lambda bi, hp, st: (bi, layer, hp, 0, 0)
    qmap = lambda bi, hp, st: (bi, st, hp)
    return pl.pallas_call(
        functools.partial(_na_lat_kernel, n_rows=n_rows),
        grid=(b, NA_H // 2, nrb // NA_SUBS),
        in_specs=[pl.BlockSpec((None, NA_SUBS * NA_QB, LANES), qmap),
                  pl.BlockSpec((None, n, LANES), kvmap),
                  pl.BlockSpec((None, n, LANES), kvmap),
                  pl.BlockSpec((None, None, None, past, LANES), cmap),
                  pl.BlockSpec((None, None, None, past, LANES), cmap),
                  pl.BlockSpec((None, 2, 3, NA_QB, NA_KW), lambda bi, hp, st: (layer, hp, 0, 0, 0))],
        out_specs=pl.BlockSpec((None, NA_SUBS * NA_QB, LANES), qmap),
        out_shape=jax.ShapeDtypeStruct((b, n, NA_W), F32),
        compiler_params=_cparams(("arbitrary", "arbitrary", "arbitrary")),
        name="na_lat",
    )(nq, nk, nv, ck, cv, bias)


def _na_ctx(nq, nk, nv):
    b, n, _ = nq.shape
    spec = pl.BlockSpec((None, n, LANES), lambda bi, hp: (bi, 0, hp))
    return pl.pallas_call(
        _na_ctx_kernel,
        grid=(b, NA_H // 2),
        in_specs=[spec, spec, spec],
        out_specs=spec,
        out_shape=jax.ShapeDtypeStruct((b, n, NA_W), F32),
        compiler_params=_cparams(("arbitrary", "arbitrary")),
        name="na_ctx",
    )(nq, nk, nv)


def _split3(x):
    hi = x.astype(BF16)
    r = x - hi.astype(F32)
    mid = r.astype(BF16)
    return hi, mid, (r - mid.astype(F32)).astype(BF16)


def _dot3_left(x, m01):
    hi, mid, lo = _split3(x)
    d = lambda a: jnp.dot(a, m01, preferred_element_type=F32)
    return (d(lo) + d(mid)) + d(hi)


def _dot3_right(m01, x):
    hi, mid, lo = _split3(x)
    d = lambda a: jnp.dot(m01, a, preferred_element_type=F32)
    return (d(lo) + d(mid)) + d(hi)


def _ssd_block(refs, sb, *, bwd, ns, cps):
    (xm_ref, xp_ref, xn_ref, z_ref, dt_ref, dtt_ref, cw_ref, cb_ref, dtb_ref, dtbt_ref, alogt_ref,
     alogx_ref, dexp_ref, nrm_ref, y_ref, xpad_ref, st_ref, yacc_ref, act_ref) = refs
    t = SSM_CHUNK
    tb = cps * t
    gw = SSM_DI // SSM_G
    hpg = SSM_H // SSM_G
    p_shift = SSM_P.bit_length() - 1
    row0 = pl.multiple_of(sb * tb, tb)

    if not bwd:
        xpad_ref[0:SUBLANES, :] = jnp.where(sb == 0, 0.0, xp_ref[...])
        xpad_ref[SUBLANES:SUBLANES + tb, :] = xm_ref[...]
        xpad_ref[SUBLANES + tb:, :] = jnp.where(sb == ns - 1, 0.0, xn_ref[...])
        conv = cb_ref[...]
        for k in range(CONV_W):
            o = SUBLANES - CONV_W // 2 + k
            conv = conv + cw_ref[k:k + 1, :] * xpad_ref[o:o + tb, :]
        act = _silu(conv)
        act_ref[pl.ds(row0, tb), :] = act
    else:
        act = act_ref[pl.ds(row0, tb), :]
    xs = act[:, :SSM_DI]
    bm = act[:, SSM_DI:SSM_DI + SSM_G * SSM_N]
    cm = act[:, SSM_DI + SSM_G * SSM_N:]

    li = lax.broadcasted_iota(jnp.int32, (t, t), 0)
    si = lax.broadcasted_iota(jnp.int32, (t, t), 1)
    keep = (si >= li) if bwd else (si <= li)
    tri = jnp.where(keep, 1.0, 0.0).astype(BF16)
    trit = jnp.where((li >= si) if bwd else (li <= si), 1.0, 0.0).astype(BF16)
    r_i = lax.broadcasted_iota(jnp.int32, (LANES, SSM_DI), 0)
    c_i = lax.broadcasted_iota(jnp.int32, (LANES, SSM_DI), 1)
    d = int(bwd)
    expand = jnp.where(r_i == d * SSM_H + (c_i >> p_shift), 1.0, 0.0).astype(BF16)
    dtv = _softplus(dt_ref[...] + dtb_ref[...])
    dtx = _dot3_left(dtv, expand)
    a_x = dtx * (-jnp.exp(alogx_ref[d:d + 1, :]))
    xdt = xs * dtx
    at = -jnp.exp(alogt_ref[...]) * _softplus(dtt_ref[...] + dtbt_ref[...])
    lane_g = lax.broadcasted_iota(jnp.int32, (1, gw), 1)

    order = range(cps - 1, -1, -1) if bwd else range(cps)
    y_rows = [None] * cps
    for ci in order:
        rows = slice(ci * t, (ci + 1) * t)
        acum = _dot3_right(tri, a_x[rows])
        tot = acum[0:1, :] if bwd else acum[t - 1:t, :]
        e_in = jnp.exp(acum)
        xdt_c = xdt[rows]
        xdt_b = xdt_c.astype(BF16)
        xout_b = (xdt_c * jnp.exp(tot - acum)).astype(BF16)
        acum_t = _dot3_left(at[:, rows], trit)
        y_parts = []
        for g in range(SSM_G):
            cg = cm[rows, g * SSM_N:(g + 1) * SSM_N].astype(BF16)
            bg = bm[rows, g * SSM_N:(g + 1) * SSM_N]
            gmat = _bdot_nt(cg, bg)
            st_g = st_ref[:, g * gw:(g + 1) * gw]
            y_g = _bdot(cg, st_g) * e_in[:, g * gw:(g + 1) * gw]
            xdt_g = xdt_b[:, g * gw:(g + 1) * gw]
            lhs, rhs = [], []
            for hh in range(hpg):
                h = g * hpg + hh
                col = acum[:, h * SSM_P:h * SSM_P + 1]
                row = acum_t[d * SSM_H + h:d * SSM_H + h + 1, :]
                lmat = jnp.exp(jnp.where(keep, col - row, -jnp.inf))
                lhs.append((gmat * lmat).astype(BF16))
                rhs.append(jnp.where((lane_g >> p_shift) == hh, xdt_g, jnp.zeros_like(xdt_g)))
            y_g = y_g + jnp.dot(jnp.concatenate(lhs, axis=1), jnp.concatenate(rhs, axis=0),
                                preferred_element_type=F32)
            y_parts.append(y_g)
            st_ref[:, g * gw:(g + 1) * gw] = (st_g * jnp.exp(tot[:, g * gw:(g + 1) * gw])
                                              + _bdot(bg.T, xout_b[:, g * gw:(g + 1) * gw]))
        y_rows[ci] = jnp.concatenate(y_parts, axis=-1)
    y_dir = jnp.concatenate(y_rows, axis=0)

    if not bwd:
        yacc_ref[pl.ds(row0, tb), :] = y_dir + xs * dexp_ref[...]
    else:
        y = (yacc_ref[pl.ds(row0, tb), :] + y_dir) * _silu(z_ref[...])
        parts = [_rms(y[:, g * gw:(g + 1) * gw], nrm_ref[:, g * gw:(g + 1) * gw]) for g in range(SSM_G)]
        y_ref[...] = jnp.concatenate(parts, axis=-1)


def _ssd_kernel(*refs, ns, cps, has_h0):
    n_in = 14
    if has_h0:
        h0f_ref, h0b_ref = refs[n_in:n_in + 2]
        refs = refs[:n_in] + refs[n_in + 2:]
    hf_ref, hb_ref = refs[n_in + 1:n_in + 3]
    block_refs = refs[:n_in + 1] + refs[n_in + 3:]
    st_ref = block_refs[n_in + 2]
    j = pl.program_id(1)

    @pl.when(j == 0)
    def _():
        st_ref[...] = h0f_ref[...] if has_h0 else jnp.zeros_like(st_ref)

    @pl.when(j == ns)
    def _():
        st_ref[...] = h0b_ref[...] if has_h0 else jnp.zeros_like(st_ref)

    @pl.when(j < ns)
    def _():
        _ssd_block(block_refs, j, bwd=False, ns=ns, cps=cps)

    @pl.when(j >= ns)
    def _():
        _ssd_block(block_refs, 2 * ns - 1 - j, bwd=True, ns=ns, cps=cps)

    @pl.when(j == ns - 1)
    def _():
        hf_ref[...] = st_ref[...]

    @pl.when(j == 2 * ns - 1)
    def _():
        hb_ref[...] = st_ref[...]


def _ssd(sz, sxbc, sdt, p, h0):
    b, n, _ = sz.shape
    t = SSM_CHUNK
    cps = min(SSD_CPS, n // t)
    tb = cps * t
    assert n % tb == 0
    ns = n // tb
    nb8 = n // SUBLANES
    tb8 = tb // SUBLANES
    has_h0 = h0 is not None
    sdt_t = jnp.swapaxes(sdt[:, :, :2 * SSM_H], 1, 2)
    sidx = lambda j: jnp.where(j >= ns, 2 * ns - 1 - j, j)
    fidx = lambda j: jnp.minimum(j, ns - 1)
    zidx = lambda j: jnp.where(j >= ns, 2 * ns - 1 - j, ns - 1)
    const2 = lambda bi, j: (0, 0)
    cw = jnp.zeros((SUBLANES, CONV_CH), F32).at[:CONV_W].set(p["ssm_conv_w"])
    dtb = jnp.zeros((1, DT_PAD), F32).at[0, :2 * SSM_H].set(p["ssm_dt_bias"].reshape(-1))
    dtbt = jnp.broadcast_to(p["ssm_dt_bias"].reshape(2 * SSM_H, 1), (2 * SSM_H, tb))
    alogt = jnp.broadcast_to(p["ssm_a_log"].reshape(2 * SSM_H, 1), (2 * SSM_H, tb))
    alogx = jnp.repeat(p["ssm_a_log"], SSM_P, axis=-1)
    dexp = jnp.repeat(p["ssm_d"], SSM_P).reshape(1, SSM_DI)
    in_specs = [pl.BlockSpec((None, tb, CONV_CH), lambda bi, j: (bi, fidx(j), 0)),
                pl.BlockSpec((None, SUBLANES, CONV_CH), lambda bi, j: (bi, jnp.maximum(fidx(j) * tb8 - 1, 0), 0)),
                pl.BlockSpec((None, SUBLANES, CONV_CH),
                             lambda bi, j: (bi, jnp.minimum((fidx(j) + 1) * tb8, nb8 - 1), 0)),
                pl.BlockSpec((None, tb, SSM_DI), lambda bi, j: (bi, zidx(j), 0)),
                pl.BlockSpec((None, tb, DT_PAD), lambda bi, j: (bi, sidx(j), 0)),
                pl.BlockSpec((None, 2 * SSM_H, tb), lambda bi, j: (bi, 0, sidx(j))),
                pl.BlockSpec((SUBLANES, CONV_CH), const2),
                pl.BlockSpec((1, CONV_CH), const2),
                pl.BlockSpec((1, DT_PAD), const2),
                pl.BlockSpec((2 * SSM_H, tb), const2),
                pl.BlockSpec((2 * SSM_H, tb), const2),
                pl.BlockSpec((2, SSM_DI), const2),
                pl.BlockSpec((1, SSM_DI), const2),
                pl.BlockSpec((1, SSM_DI), const2)]
    args = [sxbc, sxbc, sxbc, sz, sdt, sdt_t, cw, p["ssm_conv_b"].reshape(1, CONV_CH), dtb, dtbt,
            alogt, alogx, dexp, p["ssm_norm"].reshape(1, SSM_DI)]
    st_spec = pl.BlockSpec((None, SSM_N, SSM_DI), lambda bi, j: (bi, 0, 0))
    if has_h0:
        in_specs += [st_spec, st_spec]
        args += list(h0)
    y, hf, hb = pl.pallas_call(
        functools.partial(_ssd_kernel, ns=ns, cps=cps, has_h0=has_h0),
        grid=(b, 2 * ns),
        in_specs=in_specs,
        out_specs=[pl.BlockSpec((None, tb, SSM_DI), lambda bi, j: (bi, zidx(j), 0)), st_spec, st_spec],
        out_shape=[jax.ShapeDtypeStruct((b, n, SSM_DI), F32),
                   jax.ShapeDtypeStruct((b, SSM_N, SSM_DI), F32),
                   jax.ShapeDtypeStruct((b, SSM_N, SSM_DI), F32)],
        scratch_shapes=[pltpu.VMEM((tb + 2 * SUBLANES, CONV_CH), F32),
                        pltpu.VMEM((SSM_N, SSM_DI), F32),
                        pltpu.VMEM((n, SSM_DI), F32),
                        pltpu.VMEM((n, CONV_CH), F32)],
        compiler_params=_cparams(("arbitrary", "arbitrary")),
        name="ssd_lat" if has_h0 else "ssd_ctx",
    )(*args)
    return y, hf, hb


def _post_kernel(x_ref, oda_ref, ossm_ref, ona_ref, gate_ref, mod_ref, wda_ref, wssm_ref, wna_ref, wout_ref,
                 gpost_ref, gpre_ref, wr_ref, br_ref, x1_ref, h2_ref, rg_ref):
    d = x_ref.shape[1]
    tm = x_ref.shape[0]
    ts = min(POST_TS, tm)
    for i in range(tm // ts):
        r = slice(i * ts, (i + 1) * ts)
        mix = (gate_ref[r, 0:d] * _bdot(oda_ref[r, :], wda_ref[...])
               + gate_ref[r, d:2 * d] * _bdot(ossm_ref[r, :], wssm_ref[...])
               + gate_ref[r, 2 * d:3 * d] * _bdot(ona_ref[r, :], wna_ref[...]))
        mix = _bdot(mix, wout_ref[...])
        x1 = x_ref[r, :] + mod_ref[2:3, :] * _rms(mix, gpost_ref[...])
        x1_ref[r, :] = x1
        h2 = _rms(x1, gpre_ref[...]) * (1.0 + mod_ref[4:5, :]) + mod_ref[3:4, :]
        h_hi = h2.astype(BF16)
        h2_ref[r, :] = h_hi

        h_mid = (h2 - h_hi.astype(F32)).astype(BF16)
        a = jnp.dot(h_hi, wr_ref[...], preferred_element_type=F32)
        b = jnp.dot(h_mid, wr_ref[:, :LANES], preferred_element_type=F32)
        s = _sigmoid(a[:, :LANES] + (a[:, LANES:] + b))
        work = s + br_ref[...]
        col = lax.broadcasted_iota(jnp.int32, (ts, LANES), 1).astype(F32)
        picked = jnp.zeros((ts, LANES), F32)
        for _ in range(TOP_K):
            m = jnp.max(work, axis=-1, keepdims=True)
            first = jnp.min(jnp.where(work == m, col, float(LANES)), axis=-1, keepdims=True)
            hit = col == first
            picked = jnp.where(hit, 1.0, picked)
            work = jnp.where(hit, -jnp.inf, work)
        sel = picked * s
        g = sel / jnp.sum(sel, axis=-1, keepdims=True) * ROUTE_SCALE
        g_hi = g.astype(BF16).astype(F32)
        r1 = g - g_hi
        g_mid = r1.astype(BF16).astype(F32)
        g_lo = r1 - g_mid
        packed = g_hi + pltpu.roll(g_mid, N_EXPERTS, 1) + pltpu.roll(g_lo, 2 * N_EXPERTS, 1)
        rg_ref[r, :] = packed.astype(rg_ref.dtype)


def _post(x3, o_da, o_ssm, o_na, gates, mods, p):
    b, n, d = x3.shape
    tm = min(POST_TM, n)
    tps = n // tm
    t = b * n
    per_batch_mods = mods.shape[0] > 1
    mod_idx = (lambda i: (i // tps, 0, 0)) if per_batch_mods else (lambda i: (0, 0, 0))
    row = lambda i: (i, 0)
    const2 = lambda i: (0, 0)
    flat = lambda a: a.reshape(t, a.shape[-1])
    wspec = lambda k: pl.BlockSpec((k, d), const2)
    w_r = jnp.zeros((d, LANES), F32).at[:, :N_EXPERTS].set(p["w_router"])
    w_r_hi = w_r.astype(BF16)
    w_router = jnp.concatenate([w_r_hi, (w_r - w_r_hi.astype(F32)).astype(BF16)], axis=1)
    b_router = jnp.full((1, LANES), -jnp.inf, F32).at[0, :N_EXPERTS].set(p["b_router"])
    x1, h2, route = pl.pallas_call(
        _post_kernel,
        grid=(t // tm,),
        in_specs=[pl.BlockSpec((tm, d), row), pl.BlockSpec((tm, DA_W), row), pl.BlockSpec((tm, SSM_DI), row),
                  pl.BlockSpec((tm, NA_W), row), pl.BlockSpec((tm, 3 * d), row),
                  pl.BlockSpec((None, 6, d), mod_idx),
                  wspec(DA_W), wspec(SSM_DI), wspec(NA_W), wspec(d),
                  pl.BlockSpec((1, d), const2), pl.BlockSpec((1, d), const2),
                  pl.BlockSpec((d, 2 * LANES), const2), pl.BlockSpec((1, LANES), const2)],
        out_specs=[pl.BlockSpec((tm, d), row), pl.BlockSpec((tm, d), row), pl.BlockSpec((tm, LANES), row)],
        out_shape=[jax.ShapeDtypeStruct((t, d), F32), jax.ShapeDtypeStruct((t, d), BF16),
                   jax.ShapeDtypeStruct((t, LANES), BF16)],
        compiler_params=_cparams(("arbitrary",)),
        name="post",
    )(flat(x3), flat(o_da), flat(o_ssm), flat(o_na), flat(gates), mods,
      p["w_br_da"], p["w_br_ssm"], p["w_br_na"], p["w_out"],
      p["g_post1"].reshape(1, d), p["g_pre2"].reshape(1, d), w_router, b_router)
    return x1, h2, route


def _moe_kernel(x1_ref, h2_ref, rg_ref, mod_ref, wsg_ref, wsu_ref, wsd_ref, weg_ref, weu_ref, wed_ref,
                gpost_ref, o_ref, acc_ref):
    e = pl.program_id(1)
    hb = h2_ref[...]

    @pl.when(e == 0)
    def _():
        acc_ref[...] = _bdot(_silu(_bdot(hb, wsg_ref[...])) * _bdot(hb, wsu_ref[...]), wsd_ref[...])

    width = MOE_EG * EXPERT_FF
    r_i = lax.broadcasted_iota(jnp.int32, (LANES, width), 0)
    c_i = lax.broadcasted_iota(jnp.int32, (LANES, width), 1)
    ff_shift = EXPERT_FF.bit_length() - 1
    onehot = jnp.where((r_i < 3 * N_EXPERTS) & ((r_i & (N_EXPERTS - 1)) == e * MOE_EG + (c_i >> ff_shift)),
                       1.0, 0.0).astype(BF16)
    g_exp = jnp.dot(rg_ref[...], onehot, preferred_element_type=F32)
    act = _silu(_bdot(hb, weg_ref[...])) * _bdot(hb, weu_ref[...]) * g_exp
    acc_ref[...] += _bdot(act, wed_ref[...])

    @pl.when(e == pl.num_programs(1) - 1)
    def _():
        o_ref[...] = x1_ref[...] + mod_ref[5:6, :] * _rms(acc_ref[...], gpost_ref[...])


def _moe(x1, h2, route, mods, p, n):
    t, d = x1.shape
    tm = MOE_TM
    per_batch_mods = mods.shape[0] > 1
    if per_batch_mods:
        assert n % tm == 0
    tps = max(n // tm, 1)
    mod_idx = (lambda i, e: (i // tps, 0, 0)) if per_batch_mods else (lambda i, e: (0, 0, 0))
    row = lambda i, e: (i, 0)
    const2 = lambda i, e: (0, 0)
    width = MOE_EG * EXPERT_FF
    resident = dict(pipeline_mode=pl.Buffered(1)) if MOE_EG == N_EXPERTS else {}
    return pl.pallas_call(
        _moe_kernel,
        grid=(t // tm, N_EXPERTS // MOE_EG),
        in_specs=[pl.BlockSpec((tm, d), row), pl.BlockSpec((tm, d), row), pl.BlockSpec((tm, LANES), row),
                  pl.BlockSpec((None, 6, d), mod_idx),
                  pl.BlockSpec((d, SHARED_FF), const2), pl.BlockSpec((d, SHARED_FF), const2),
                  pl.BlockSpec((SHARED_FF, d), const2),
                  pl.BlockSpec((d, width), lambda i, e: (0, e), **resident),
                  pl.BlockSpec((d, width), lambda i, e: (0, e), **resident),
                  pl.BlockSpec((width, d), lambda i, e: (e, 0), **resident),
                  pl.BlockSpec((1, d), const2)],
        out_specs=pl.BlockSpec((tm, d), row),
        out_shape=jax.ShapeDtypeStruct((t, d), F32),
        scratch_shapes=[pltpu.VMEM((tm, d), F32)],
        compiler_params=_cparams(("arbitrary", "arbitrary")),
        name="moe",
    )(x1, h2, route, mods,
      p["w_s_gate"], p["w_s_up"], p["w_s_down"], p["w_e_gate"], p["w_e_up"], p["w_e_down"],
      p["g_post2"].reshape(1, d))


def _rope_tables(n):
    pos = jnp.arange(n)
    rows = (pos // GRID_W).astype(F32)
    cols = (pos % GRID_W).astype(F32)
    n_freq = DA_HD // 4
    inv = ROPE_BASE ** (-jnp.arange(n_freq, dtype=F32) / n_freq)
    ar, ac = rows[:, None] * inv, cols[:, None] * inv
    zero = jnp.zeros_like(ar)

    def lanes(first_r, second_r, first_c, second_c):
        half = jnp.concatenate([first_r, second_r, first_c, second_c], axis=-1)
        return jnp.concatenate([half, half], axis=-1)

    cos = lanes(jnp.cos(ar), jnp.cos(ar), jnp.cos(ac), jnp.cos(ac))
    s_first = lanes(-jnp.sin(ar), zero, -jnp.sin(ac), zero)
    s_second = lanes(zero, jnp.sin(ar), zero, jnp.sin(ac))
    return cos, s_first, s_second


def _prep_layer_params(stacked, layer):
    p = {name: arr[layer] for name, arr in stacked.items()}
    w_in = p["w_in"]
    dt0 = 3 * DA_W + SSM_DI + CONV_CH
    d = w_in.shape[0]
    p["w_in_p"] = jnp.concatenate(
        [w_in[:, :dt0], w_in[:, dt0 + 2 * SSM_H:], w_in[:, dt0:dt0 + 2 * SSM_H],
         jnp.zeros((d, DT_PAD - 2 * SSM_H), w_in.dtype)], axis=-1).astype(BF16)
    for name in ("w_gate", "w_br_da", "w_br_ssm", "w_br_na", "w_out", "w_s_gate", "w_s_up", "w_s_down"):
        p[name] = p[name].astype(BF16)
    ff = N_EXPERTS * EXPERT_FF
    p["w_e_gate"] = p["w_e_gate"].astype(BF16).transpose(1, 0, 2).reshape(d, ff)
    p["w_e_up"] = p["w_e_up"].astype(BF16).transpose(1, 0, 2).reshape(d, ff)
    p["w_e_down"] = p["w_e_down"].astype(BF16).reshape(ff, d)
    return p


def _trunk_layer(x3, mods, p, layer, ctx):
    b, n, d = x3.shape
    latent = ctx is not None
    rope_tabs = ctx[6] if latent else None
    qkv_dtype = BF16 if latent else F32
    outs = _pre(x3, mods, p["g_pre1"], p["w_in_p"], p["w_gate"], rope_tabs, qkv_dtype)
    dq, dk, dv, sz, sxbc, nq, nk, nv, sdt, gates = outs[:10]
    if latent:
        ck, cv, cnk, cnv, h0f, h0b, _, na_bias = ctx
        o_da = _da(dq, dk, dv, p["da_lambda"], p["da_subln"], layer, (ck, cv, outs[10]))
        o_na = _na_lat(nq, nk, nv, cnk, cnv, na_bias, layer)
        o_ssm, hf, hb = _ssd(sz, sxbc, sdt, p, (h0f, h0b))
    else:
        o_da = _da(dq, dk, dv, p["da_lambda"], p["da_subln"], layer, None)
        o_na = _na_ctx(nq, nk, nv)
        o_ssm, hf, hb = _ssd(sz, sxbc, sdt, p, None)
    x1, h2, route = _post(x3, o_da, o_ssm, o_na, gates, mods, p)
    x2 = _moe(x1, h2, route, mods, p, n).reshape(b, n, d)
    state = None
    if not latent:
        to_state = lambda s: s.reshape(b, SSM_N, SSM_H, SSM_P).transpose(0, 2, 3, 1)
        state = (dk.reshape(b, n, DA_H, 2, DA_HD).transpose(0, 2, 1, 3, 4),
                 dv.reshape(b, n, DA_H, 2 * DA_HD).transpose(0, 2, 1, 3),
                 nk.reshape(b, n, NA_H, NA_HD).transpose(0, 2, 1, 3),
                 nv.reshape(b, n, NA_H, NA_HD).transpose(0, 2, 1, 3),
                 to_state(hf), to_state(hb))
    return x2, state


def kernel(x_prompt, x_sample, c, cache_da_k, cache_da_v, cache_na_k, cache_na_v, state_ssm_f, state_ssm_b,
           c_ctx, w_ada, b_ada, g_pre1, g_post1, g_pre2, g_post2, w_in, w_gate, da_lambda, da_subln,
           ssm_conv_w, ssm_conv_b, ssm_dt_bias, ssm_a_log, ssm_d, ssm_norm, na_rpb, w_br_da, w_br_ssm,
           w_br_na, w_out, w_router, b_router, w_e_gate, w_e_up, w_e_down, w_s_gate, w_s_up, w_s_down):
    stacked = {
        "g_pre1": g_pre1, "g_post1": g_post1, "g_pre2": g_pre2, "g_post2": g_post2, "w_in": w_in,
        "w_gate": w_gate, "da_lambda": da_lambda, "da_subln": da_subln, "ssm_conv_w": ssm_conv_w,
        "ssm_conv_b": ssm_conv_b, "ssm_dt_bias": ssm_dt_bias, "ssm_a_log": ssm_a_log, "ssm_d": ssm_d,
        "ssm_norm": ssm_norm, "na_rpb": na_rpb, "w_br_da": w_br_da, "w_br_ssm": w_br_ssm, "w_br_na": w_br_na,
        "w_out": w_out, "w_router": w_router, "b_router": b_router, "w_e_gate": w_e_gate, "w_e_up": w_e_up,
        "w_e_down": w_e_down, "w_s_gate": w_s_gate, "w_s_up": w_s_up, "w_s_down": w_s_down,
    }
    depth = w_ada.shape[0]
    nb = x_sample.shape[0]
    d = x_prompt.shape[-1]
    rows = -(-(1 + nb) // SUBLANES) * SUBLANES
    cvec = jnp.zeros((rows, d), F32).at[0].set(c_ctx).at[1:1 + nb].set(c)
    mods = _ada(cvec, w_ada, b_ada).reshape(depth, rows, 6, d)

    past = cache_da_k.shape[3]
    ck_all = cache_da_k.reshape(nb, depth, DA_H, past, 2 * DA_HD)
    pair = lambda a: a.reshape(nb, depth, NA_H // 2, 2, past, NA_HD).transpose(0, 1, 2, 4, 3, 5).reshape(
        nb, depth, NA_H // 2, past, 2 * NA_HD)
    cnk_all, cnv_all = pair(cache_na_k), pair(cache_na_v)
    st_t = lambda s: s.transpose(0, 1, 4, 2, 3).reshape(nb, depth, SSM_N, SSM_DI)
    h0f_all, h0b_all = st_t(state_ssm_f), st_t(state_ssm_b)

    n_lat = x_sample.shape[1]
    rope_tabs = _rope_tables(n_lat)
    na_bias = _na_bias_tables(na_rpb.reshape((depth * NA_H,) + na_rpb.shape[2:]), n_lat // GRID_W)
    na_bias = na_bias.reshape((depth, NA_H) + na_bias.shape[1:])

    xp, xs = x_prompt, x_sample
    new = ([], [], [], [], [], [])
    for layer in range(depth):
        p = _prep_layer_params(stacked, layer)
        xp, st = _trunk_layer(xp, mods[layer, 0:1], p, layer, None)
        for lst, arr in zip(new, st):
            lst.append(arr)
        ctx = (ck_all, cache_da_v, cnk_all, cnv_all, h0f_all[:, layer], h0b_all[:, layer], rope_tabs, na_bias)
        xs, _ = _trunk_layer(xs, mods[layer, 1:1 + nb], p, layer, ctx)
    return (xp, xs) + tuple(jnp.stack(lst, axis=1) for lst in new)
```

```python
import functools
import math

import numpy as np
import jax
import jax.numpy as jnp
from jax import lax
from jax.experimental import pallas as pl
from jax.experimental.pallas import tpu as pltpu

F32 = jnp.float32
BF16 = jnp.bfloat16

D_MODEL = 1024
GRID_W = 64
EPS = 1e-6
DA_H, DA_HD = 4, 64
DA_W = DA_H * 2 * DA_HD
ROPE_BASE = 10000.0
SSM_H, SSM_P, SSM_G, SSM_N = 8, 64, 2, 128
SSM_DI = SSM_H * SSM_P
SSM_CHUNK = 128
CONV_W = 5
CONV_CH = SSM_DI + 2 * SSM_G * SSM_N
NA_H, NA_HD = 8, 64
NA_W = NA_H * NA_HD
WIN_R, WIN_C = 8, 16
N_EXPERTS, TOP_K, EXPERT_FF, SHARED_FF = 32, 4, 128, 256
ROUTE_SCALE = 2.5

LANES = 128
SUBLANES = 8
VMEM_LIMIT_BYTES = 56 * 2**20

PRE_TM = 256
POST_TM = 512
MOE_TM = 256
MOE_EG = 32
DA_TQ = 1024
DA_TS = 256
SSD_CPS = 4
DA_BOUND_SLACK = 1.0 + 2.0 ** -6
DA_MIN_L = 2.0 ** -60
DA_TK = 512
NA_QROWS = WIN_R // 2
NA_QB = NA_QROWS * GRID_W
NA_SUBS = 8
NA_KROWS = NA_QROWS + WIN_R
NA_KW = NA_KROWS * GRID_W
DT_PAD = LANES

OFF_DQ, OFF_DK, OFF_DV, OFF_SZ, OFF_XBC, OFF_NQ, OFF_NK, OFF_NV, OFF_DT = (
    0, 512, 1024, 1536, 2048, 3072, 3584, 4096, 4608)
IN_WP = OFF_DT + DT_PAD

_NT = (((1,), (1,)), ((), ()))
LOG2E = math.log2(math.e)
NA_QSCALE = NA_HD ** -0.5 * LOG2E


def _cparams(sem):
    return pltpu.CompilerParams(dimension_semantics=sem, vmem_limit_bytes=VMEM_LIMIT_BYTES)


def _bdot(a, b):
    return jnp.dot(a.astype(BF16), b.astype(BF16), preferred_element_type=F32)


def _bdot_nt(a, b):
    return lax.dot_general(a.astype(BF16), b.astype(BF16), _NT, preferred_element_type=F32)


def _sigmoid(x):
    return 1.0 / (1.0 + jnp.exp(-x))


def _silu(x):
    return x * _sigmoid(x)


def _softplus(x):
    return jnp.maximum(x, 0.0) + jnp.log(1.0 + jnp.exp(-jnp.abs(x)))


def _rms(x, g):
    return x * lax.rsqrt(jnp.mean(x * x, axis=-1, keepdims=True) + EPS) * g


def _ada_kernel(c_ref, w_ref, b_ref, o_ref):
    o_ref[...] = _bdot(_silu(c_ref[...]), w_ref[...]) + b_ref[...]


def _ada(cvec, w_ada, b_ada):
    nl, d, d6 = w_ada.shape
    r = cvec.shape[0]
    tn = 1024
    return pl.pallas_call(
        _ada_kernel,
        grid=(nl, d6 // tn),
        in_specs=[pl.BlockSpec((r, d), lambda l, j: (0, 0)),
                  pl.BlockSpec((None, d, tn), lambda l, j: (l, 0, j)),
                  pl.BlockSpec((None, 1, tn), lambda l, j: (l, 0, j))],
        out_specs=pl.BlockSpec((None, r, tn), lambda l, j: (l, 0, j)),
        out_shape=jax.ShapeDtypeStruct((nl, r, d6), F32),
        compiler_params=_cparams(("arbitrary", "arbitrary")),
        name="ada",
    )(cvec, w_ada, b_ada.reshape(nl, 1, d6))


def _pre_kernel(*refs, rope):
    if rope:
        x_ref, mod_ref, g_ref, win_ref, wg_ref, cos_ref, sa_ref, sb_ref = refs[:8]
        outs = refs[8:]
    else:
        x_ref, mod_ref, g_ref, win_ref, wg_ref = refs[:5]
        outs = refs[5:]
    dq_ref, dk_ref, dv_ref, sz_ref, xbc_ref, nq_ref, nk_ref, nv_ref, dt_ref, gate_ref = outs[:10]
    x = x_ref[...]
    h = _rms(x, g_ref[...]) * (1.0 + mod_ref[1:2, :]) + mod_ref[0:1, :]
    hb = h.astype(BF16)

    def proj(off, width):
        return jnp.dot(hb, win_ref[:, off:off + width], preferred_element_type=F32)

    lane = lax.broadcasted_iota(jnp.int32, (1, LANES), 1)
    for off, ref in ((OFF_DQ, dq_ref), (OFF_DK, dk_ref)):
        if rope:
            cos, sa, sb = cos_ref[...], sa_ref[...], sb_ref[...]
            kn = jnp.zeros((x.shape[0], LANES), F32)
            for hd in range(DA_H):
                u = proj(off + hd * LANES, LANES)
                u = u * cos + pltpu.roll(u, LANES - 16, 1) * sa + pltpu.roll(u, 16, 1) * sb
                ub = u.astype(ref.dtype)
                ref[:, hd * LANES:(hd + 1) * LANES] = ub
                if ref is dk_ref:
                    sq = ub.astype(F32) * ub.astype(F32)
                    for t in (0, 1):
                        nt = jnp.sum(jnp.where((lane >= DA_HD) == bool(t), sq, 0.0), axis=-1, keepdims=True)
                        kn = jnp.where(lane == 2 * hd + t, nt, kn)
            if ref is dk_ref:
                outs[10][...] = kn
        else:
            ref[...] = proj(off, DA_W).astype(ref.dtype)
    dv_ref[...] = proj(OFF_DV, DA_W).astype(dv_ref.dtype)
    sz_ref[...] = proj(OFF_SZ, SSM_DI)
    xbc_ref[...] = proj(OFF_XBC, CONV_CH)
    nq_ref[...] = proj(OFF_NQ, NA_W).astype(nq_ref.dtype)
    nk_ref[...] = proj(OFF_NK, NA_W).astype(nk_ref.dtype)
    nv_ref[...] = proj(OFF_NV, NA_W).astype(nv_ref.dtype)
    dt_ref[...] = proj(OFF_DT, DT_PAD)
    gate_ref[...] = _sigmoid(jnp.dot(hb, wg_ref[...], preferred_element_type=F32))


def _pre(x3, mods, g_pre1, w_in_p, w_gate, rope_tabs, qkv_dtype):
    b, n, d = x3.shape
    tm = PRE_TM
    tps = max(n // tm, 1)
    t = b * n
    x2 = x3.reshape(t, d)
    per_batch_mods = mods.shape[0] > 1
    assert n % tm == 0 or (not per_batch_mods and rope_tabs is None and tm % n == 0)
    mod_idx = (lambda i: (i // tps, 0, 0)) if per_batch_mods else (lambda i: (0, 0, 0))
    const2 = lambda i: (0, 0)
    row = lambda i: (i, 0)
    in_specs = [pl.BlockSpec((tm, d), row),
                pl.BlockSpec((None, 6, d), mod_idx),
                pl.BlockSpec((1, d), const2),
                pl.BlockSpec((d, IN_WP), const2, pipeline_mode=pl.Buffered(1)),
                pl.BlockSpec((d, 3 * d), const2, pipeline_mode=pl.Buffered(1))]
    args = [x2, mods, g_pre1.reshape(1, d), w_in_p, w_gate]
    rope = rope_tabs is not None
    if rope:
        in_specs += [pl.BlockSpec((tm, LANES), lambda i: (i % tps, 0))] * 3
        args += list(rope_tabs)
    widths = (DA_W, DA_W, DA_W, SSM_DI, CONV_CH, NA_W, NA_W, NA_W, DT_PAD, 3 * d)
    dtypes = (qkv_dtype, qkv_dtype, qkv_dtype, F32, F32, qkv_dtype, qkv_dtype, qkv_dtype, F32, F32)
    if rope:
        widths += (LANES,)
        dtypes += (F32,)
    outs = pl.pallas_call(
        functools.partial(_pre_kernel, rope=rope),
        grid=(t // tm,),
        in_specs=in_specs,
        out_specs=[pl.BlockSpec((tm, w), row) for w in widths],
        out_shape=[jax.ShapeDtypeStruct((t, w), dt) for w, dt in zip(widths, dtypes)],
        compiler_params=_cparams(("arbitrary",)),
        name="pre_rope" if rope else "pre",
    )(*args)
    return [o.reshape(b, n, o.shape[-1]) for o in outs]


def _da_kernel(*refs, lam_init, has_ctx, tk):
    if has_ctx:
        (lam_ref, sub_ref, q_ref, k_ref, v_ref, ck_ref, cv_ref, kn_ref, o_ref, s_ref, p_ref, sc_ref, pc_ref,
         ku_ref, pf_ref, pcf_ref) = refs
    else:
        lam_ref, sub_ref, q_ref, k_ref, v_ref, o_ref, s_ref, p_ref = refs
    _, _, ts, _ = p_ref.shape
    nsub = q_ref.shape[0] // ts
    nc = k_ref.shape[0] // tk
    lf = lam_ref[...]
    lam = (jnp.exp(jnp.sum(lf[0:1] * lf[1:2], axis=-1, keepdims=True))
           - jnp.exp(jnp.sum(lf[2:3] * lf[3:4], axis=-1, keepdims=True)) + lam_init)
    lane = lax.broadcasted_iota(jnp.int32, (1, LANES), 1)

    def fold(acc, x, op):
        for j in range(x.shape[1] // LANES):
            acc = op(acc, x[:, j * LANES:(j + 1) * LANES])
        return acc

    def scores(i, t):
        q = q_ref[i * ts:(i + 1) * ts, :].astype(F32) * (DA_HD ** -0.5 * LOG2E)
        qt = jnp.where((lane >= DA_HD) == bool(t), q, 0.0).astype(BF16)
        m_acc = jnp.full((ts, LANES), -jnp.inf, F32)
        for kc in range(nc):
            s = _bdot_nt(qt, k_ref[kc * tk:(kc + 1) * tk, :])
            s_ref[t, kc] = s
            m_acc = fold(m_acc, s, jnp.maximum)
        if has_ctx:
            sc = _bdot_nt(qt, ck_ref[...])
            sc_ref[t] = sc
            m_acc = fold(m_acc, sc, jnp.maximum)
        return jnp.max(m_acc, axis=-1, keepdims=True)

    def exps(i, t, m):
        l_acc = jnp.zeros((ts, LANES), F32)
        for kc in range(nc):
            p = jnp.exp2(s_ref[t, kc] - m)
            p_ref[t, kc] = p.astype(BF16)
            l_acc = fold(l_acc, p, jnp.add)
        if has_ctx:
            pc = jnp.exp2(sc_ref[t] - m)
            pc_ref[t] = pc.astype(BF16)
            l_acc = fold(l_acc, pc, jnp.add)
        return jnp.sum(l_acc, axis=-1, keepdims=True)

    def attend(i, l0, l1, p_src, pc_src):
        r = (lam * l0 / l1).astype(BF16)
        acc = jnp.zeros((ts, LANES), F32)
        for kc in range(nc):
            acc = acc + jnp.dot(p_src[0, kc] - p_src[1, kc] * r,
                                v_ref[kc * tk:(kc + 1) * tk, :].astype(BF16), preferred_element_type=F32)
        if has_ctx:
            acc = acc + jnp.dot(pc_src[0] - pc_src[1] * r, cv_ref[...].astype(BF16),
                                preferred_element_type=F32)
        o = _rms(acc * (1.0 / l0), sub_ref[...]) * (1.0 - lam_init)
        o_ref[i * ts:(i + 1) * ts, :] = o.astype(o_ref.dtype)

    def exact_step():
        for i in range(nsub):
            l0 = exps(i, 0, scores(i, 0))
            l1 = exps(i, 1, scores(i, 1))
            attend(i, l0, l1, p_ref, pc_ref if has_ctx else None)

    if not has_ctx:
        exact_step()
        return

    @pl.when(pl.program_id(2) == 0)
    def _():
        hd = pl.program_id(1)
        kmax = jnp.max(kn_ref[...], axis=0, keepdims=True)
        ckf = ck_ref[...].astype(BF16).astype(F32)
        c2 = ckf * ckf
        for t in (0, 1):
            k_lat = jnp.max(jnp.where(lane == 2 * hd + t, kmax, 0.0), axis=-1, keepdims=True)
            k_ctx = jnp.max(jnp.sum(jnp.where((lane >= DA_HD) == bool(t), c2, 0.0), axis=-1, keepdims=True),
                            axis=0, keepdims=True)
            ku_ref[t] = jnp.broadcast_to(jnp.sqrt(jnp.maximum(k_lat, k_ctx)), ku_ref.shape[1:])

    def scores_exps(i, t):
        q = q_ref[i * ts:(i + 1) * ts, :].astype(F32) * (DA_HD ** -0.5 * LOG2E)
        qt = jnp.where((lane >= DA_HD) == bool(t), q, 0.0).astype(BF16)
        qf = qt.astype(F32)
        u = jnp.sqrt(jnp.sum(qf * qf, axis=-1, keepdims=True)) * (ku_ref[t, 0:1, 0:1] * DA_BOUND_SLACK)
        l_acc = jnp.zeros((ts, LANES), F32)
        for kc in range(nc):
            p = jnp.exp2(_bdot_nt(qt, k_ref[kc * tk:(kc + 1) * tk, :]) - u)
            pf_ref[i, t, kc] = p.astype(BF16)
            l_acc = fold(l_acc, p, jnp.add)
        pc = jnp.exp2(_bdot_nt(qt, ck_ref[...]) - u)
        pcf_ref[i, t] = pc.astype(BF16)
        l_acc = fold(l_acc, pc, jnp.add)
        return jnp.sum(l_acc, axis=-1, keepdims=True)

    ls = {}
    for i in range(nsub + 1):
        if i < nsub:
            ls[i, 0] = scores_exps(i, 0)
        if i > 0:
            attend(i - 1, ls[i - 1, 0], ls[i - 1, 1], pf_ref.at[i - 1], pcf_ref.at[i - 1])
        if i < nsub:
            ls[i, 1] = scores_exps(i, 1)
    low = functools.reduce(jnp.minimum, ls.values())
    row_sums_ok = jnp.min(low) >= DA_MIN_L

    @pl.when(jnp.logical_not(row_sums_ok))
    def _():
        exact_step()


def _da(dq, dk, dv, lam_p, subln, layer, ctx_kv):
    b, n, _ = dq.shape
    lam_init = 0.8 - 0.6 * math.exp(-0.3 * layer)
    tq = min(DA_TQ, n)
    ts = min(DA_TS, tq)
    tk = min(DA_TK, n)
    has_ctx = ctx_kv is not None
    qmap = lambda bi, h, qi: (bi, qi, h)
    kvmap = lambda bi, h, qi: (bi, 0, h)
    in_specs = [pl.BlockSpec((4, DA_HD), lambda bi, h, qi: (0, 0)),
                pl.BlockSpec((1, 2 * DA_HD), lambda bi, h, qi: (0, 0)),
                pl.BlockSpec((None, tq, LANES), qmap),
                pl.BlockSpec((None, n, LANES), kvmap),
                pl.BlockSpec((None, n, LANES), kvmap)]
    args = [lam_p, subln.reshape(1, 2 * DA_HD), dq, dk, dv]
    scratch = [pltpu.VMEM((2, n // tk, ts, tk), F32), pltpu.VMEM((2, n // tk, ts, tk), BF16)]
    if has_ctx:
        ck, cv, kn = ctx_kv
        past = ck.shape[3]
        cmap = lambda bi, h, qi: (bi, layer, h, 0, 0)
        in_specs += [pl.BlockSpec((None, None, None, past, LANES), cmap)] * 2
        in_specs += [pl.BlockSpec((None, n, LANES), lambda bi, h, qi: (bi, 0, 0))]
        args += [ck, cv, kn]
        scratch += [pltpu.VMEM((2, ts, past), F32), pltpu.VMEM((2, ts, past), BF16),
                    pltpu.VMEM((2, SUBLANES, LANES), F32),
                    pltpu.VMEM((tq // ts, 2, n // tk, ts, tk), BF16), pltpu.VMEM((tq // ts, 2, ts, past), BF16)]
    return pl.pallas_call(
        functools.partial(_da_kernel, lam_init=lam_init, has_ctx=has_ctx, tk=tk),
        grid=(b, DA_H, n // tq),
        in_specs=in_specs,
        out_specs=pl.BlockSpec((None, tq, LANES), qmap),
        out_shape=jax.ShapeDtypeStruct((b, n, DA_W), F32),
        scratch_shapes=scratch,
        compiler_params=_cparams(("arbitrary", "arbitrary", "arbitrary")),
        name="da_lat" if has_ctx else "da_ctx",
    )(*args)


def _pair_attend(q, sources):
    lane = lax.broadcasted_iota(jnp.int32, (1, LANES), 1)
    outs = []
    for j in (0, 1):
        qj = jnp.where((lane >= NA_HD) == bool(j), q, 0.0).astype(BF16)
        ss = []
        for k, _, bias in sources:
            s = _bdot_nt(qj, k)
            if bias is not None:
                s = s + bias(j)
            ss.append(s)
        m = ss[0].max(axis=-1, keepdims=True)
        for s in ss[1:]:
            m = jnp.maximum(m, s.max(axis=-1, keepdims=True))
        l = 0.0
        o = 0.0
        for s, (_, v, _) in zip(ss, sources):
            p = jnp.exp2(s - m)
            l = l + p.sum(axis=-1, keepdims=True)
            o = o + _bdot(p, v)
        outs.append(o * (1.0 / l))
    return jnp.where(lane >= NA_HD, outs[1], outs[0])


def _na_lat_kernel(q_ref, k_ref, v_ref, ck_ref, cv_ref, b_ref, o_ref, *, n_rows):
    step = pl.program_id(2)
    n_blocks = n_rows // NA_QROWS
    ck, cv = ck_ref[...], cv_ref[...]
    for sub in range(NA_SUBS):
        blk = step * NA_SUBS + sub
        var = jnp.where(blk == 0, 0, jnp.where(blk == n_blocks - 1, 2, 1))
        row0 = jnp.clip(blk * NA_QROWS - WIN_R // 2, 0, n_rows - NA_KROWS)
        start = pl.multiple_of(row0 * GRID_W, GRID_W)
        kw = k_ref[pl.ds(start, NA_KW), :]
        vw = v_ref[pl.ds(start, NA_KW), :]
        rows = slice(sub * NA_QB, (sub + 1) * NA_QB)
        q = q_ref[rows, :].astype(F32) * NA_QSCALE
        bias = functools.partial(lambda j, var: b_ref[j, var], var=var)
        o_ref[rows, :] = _pair_attend(q, [(kw, vw, bias), (ck, cv, None)])


def _na_ctx_kernel(q_ref, k_ref, v_ref, o_ref):
    q = q_ref[...].astype(F32) * NA_QSCALE
    o_ref[...] = _pair_attend(q, [(k_ref[...], v_ref[...], None)])


def _na_bias_tables(rpb, n_rows):
    nrb = n_rows // NA_QROWS
    nh = rpb.shape[0]
    pad = GRID_W - WIN_C
    rpad = jnp.pad(rpb.astype(F32), ((0, 0), (0, 0), (pad, pad)))
    toep = jnp.stack([rpad[:, :, GRID_W - 1 - qc:2 * GRID_W - 1 - qc] for qc in range(GRID_W)], axis=2)
    cols = np.arange(GRID_W)
    s_c = np.clip(cols - WIN_C // 2, 0, GRID_W - WIN_C)[:, None]
    ok_c = (cols[None, :] >= s_c) & (cols[None, :] < s_c + WIN_C)
    toep = jnp.where(jnp.asarray(ok_c), toep * LOG2E, -jnp.inf)
    n_off = 2 * WIN_R
    neg = jnp.full_like(toep, -jnp.inf)
    pad_off = lambda x: jnp.pad(x, ((0, 0), (0, n_off - x.shape[1]), (0, 0), (0, 0)), constant_values=-jnp.inf)
    tiles = jnp.stack([pad_off(jnp.concatenate([toep[:, :-1], toep[:, 1:]], axis=-1)),
                       pad_off(jnp.concatenate([toep, neg], axis=-1)),
                       pad_off(jnp.concatenate([neg, toep], axis=-1))], axis=1)
    plans = []
    for rb in (0, 1, nrb - 1):
        row0 = min(max(rb * NA_QROWS - WIN_R // 2, 0), n_rows - NA_KROWS)
        q_rows = rb * NA_QROWS + np.arange(NA_QROWS)
        k_rows = row0 + np.arange(NA_KROWS)
        s_r = np.clip(q_rows - WIN_R // 2, 0, n_rows - WIN_R)[:, None]
        ok_r = (k_rows[None, :] >= s_r) & (k_rows[None, :] < s_r + WIN_R)
        off_r = np.clip(k_rows[None, :] - q_rows[:, None] + WIN_R - 1, 0, 2 * WIN_R - 2)
        plan = []
        for qr in range(NA_QROWS):
            for kp in range(NA_KROWS // 2):
                left, right = bool(ok_r[qr, 2 * kp]), bool(ok_r[qr, 2 * kp + 1])
                if left and right:
                    assert off_r[qr, 2 * kp + 1] == off_r[qr, 2 * kp] + 1
                    plan.append((0, int(off_r[qr, 2 * kp])))
                elif left:
                    plan.append((1, int(off_r[qr, 2 * kp])))
                elif right:
                    plan.append((2, int(off_r[qr, 2 * kp + 1])))
                else:
                    plan.append(None)
        plans.append(plan)

    def assemble(t_ref, o_ref):
        for v, plan in enumerate(plans):
            for idx, entry in enumerate(plan):
                qr, kp = divmod(idx, NA_KROWS // 2)
                tile = (jnp.full((GRID_W, 2 * GRID_W), -jnp.inf, F32) if entry is None
                        else t_ref[entry[0], entry[1]])
                o_ref[v, qr * GRID_W:(qr + 1) * GRID_W, kp * 2 * GRID_W:(kp + 1) * 2 * GRID_W] = tile

    return pl.pallas_call(
        assemble,
        grid=(nh,),
        in_specs=[pl.BlockSpec((None, 3, n_off, GRID_W, 2 * GRID_W), lambda h: (h, 0, 0, 0, 0))],
        out_specs=pl.BlockSpec((None, 3, NA_QB, NA_KW), lambda h: (h, 0, 0, 0)),
        out_shape=jax.ShapeDtypeStruct((nh, 3, NA_QB, NA_KW), F32),
        compiler_params=_cparams(("arbitrary",)),
        name="na_bias",
    )(tiles)


def _na_lat(nq, nk, nv, ck, cv, bias, layer):
    b, n, _ = nq.shape
    n_rows = n // GRID_W
    assert n_rows >= 2 * NA_KROWS and n_rows % (NA_QROWS * NA_SUBS) == 0 and NA_QROWS == WIN_R // 2
    nrb = n_rows // NA_QROWS
    past = ck.shape[3]
    kvmap = lambda bi, hp, st: (bi, 0, hp)
    cmap = lambda bi, hp, st: (bi, layer, hp, 0, 0)
    qmap = lambda bi, hp, st: (bi, st, hp)
    return pl.pallas_call(
        functools.partial(_na_lat_kernel, n_rows=n_rows),
        grid=(b, NA_H // 2, nrb // NA_SUBS),
        in_specs=[pl.BlockSpec((None, NA_SUBS * NA_QB, LANES), qmap),
                  pl.BlockSpec((None, n, LANES), kvmap),
                  pl.BlockSpec((None, n, LANES), kvmap),
                  pl.BlockSpec((None, None, None, past, LANES), cmap),
                  pl.BlockSpec((None, None, None, past, LANES), cmap),
                  pl.BlockSpec((None, 2, 3, NA_QB, NA_KW), lambda bi, hp, st: (layer, hp, 0, 0, 0))],
        out_specs=pl.BlockSpec((None, NA_SUBS * NA_QB, LANES), qmap),
        out_shape=jax.ShapeDtypeStruct((b, n, NA_W), F32),
        compiler_params=_cparams(("arbitrary", "arbitrary", "arbitrary")),
        name="na_lat",
    )(nq, nk, nv, ck, cv, bias)


def _na_ctx(nq, nk, nv):
    b, n, _ = nq.shape
    spec = pl.BlockSpec((None, n, LANES), lambda bi, hp: (bi, 0, hp))
    return pl.pallas_call(
        _na_ctx_kernel,
        grid=(b, NA_H // 2),
        in_specs=[spec, spec, spec],
        out_specs=spec,
        out_shape=jax.ShapeDtypeStruct((b, n, NA_W), F32),
        compiler_params=_cparams(("arbitrary", "arbitrary")),
        name="na_ctx",
    )(nq, nk, nv)


def _split3(x):
    hi = x.astype(BF16)
    r = x - hi.astype(F32)
    mid = r.astype(BF16)
    return hi, mid, (r - mid.astype(F32)).astype(BF16)


def _dot3_left(x, m01):
    hi, mid, lo = _split3(x)
    d = lambda a: jnp.dot(a, m01, preferred_element_type=F32)
    return (d(lo) + d(mid)) + d(hi)


def _dot3_right(m01, x):
    hi, mid, lo = _split3(x)
    d = lambda a: jnp.dot(m01, a, preferred_element_type=F32)
    return (d(lo) + d(mid)) + d(hi)


def _ssd_block(refs, sb, *, bwd, ns, cps):
    (xm_ref, xp_ref, xn_ref, z_ref, dt_ref, dtt_ref, cw_ref, cb_ref, dtb_ref, dtbt_ref, alogt_ref,
     alogx_ref, dexp_ref, nrm_ref, y_ref, xpad_ref, st_ref, yacc_ref, act_ref) = refs
    t = SSM_CHUNK
    tb = cps * t
    gw = SSM_DI // SSM_G
    hpg = SSM_H // SSM_G
    p_shift = SSM_P.bit_length() - 1
    row0 = pl.multiple_of(sb * tb, tb)

    if not bwd:
        xpad_ref[0:SUBLANES, :] = jnp.where(sb == 0, 0.0, xp_ref[...])
        xpad_ref[SUBLANES:SUBLANES + tb, :] = xm_ref[...]
        xpad_ref[SUBLANES + tb:, :] = jnp.where(sb == ns - 1, 0.0, xn_ref[...])
        conv = cb_ref[...]
        for k in range(CONV_W):
            o = SUBLANES - CONV_W // 2 + k
            conv = conv + cw_ref[k:k + 1, :] * xpad_ref[o:o + tb, :]
        act = _silu(conv)
        act_ref[pl.ds(row0, tb), :] = act
    else:
        act = act_ref[pl.ds(row0, tb), :]
    xs = act[:, :SSM_DI]
    bm = act[:, SSM_DI:SSM_DI + SSM_G * SSM_N]
    cm = act[:, SSM_DI + SSM_G * SSM_N:]

    li = lax.broadcasted_iota(jnp.int32, (t, t), 0)
    si = lax.broadcasted_iota(jnp.int32, (t, t), 1)
    keep = (si >= li) if bwd else (si <= li)
    tri = jnp.where(keep, 1.0, 0.0).astype(BF16)
    trit = jnp.where((li >= si) if bwd else (li <= si), 1.0, 0.0).astype(BF16)
    r_i = lax.broadcasted_iota(jnp.int32, (LANES, SSM_DI), 0)
    c_i = lax.broadcasted_iota(jnp.int32, (LANES, SSM_DI), 1)
    d = int(bwd)
    expand = jnp.where(r_i == d * SSM_H + (c_i >> p_shift), 1.0, 0.0).astype(BF16)
    dtv = _softplus(dt_ref[...] + dtb_ref[...])
    dtx = _dot3_left(dtv, expand)
    a_x = dtx * (-jnp.exp(alogx_ref[d:d + 1, :]))
    xdt = xs * dtx
    at = -jnp.exp(alogt_ref[...]) * _softplus(dtt_ref[...] + dtbt_ref[...])
    lane_g = lax.broadcasted_iota(jnp.int32, (1, gw), 1)

    order = range(cps - 1, -1, -1) if bwd else range(cps)
    y_rows = [None] * cps
    for ci in order:
        rows = slice(ci * t, (ci + 1) * t)
        acum = _dot3_right(tri, a_x[rows])
        tot = acum[0:1, :] if bwd else acum[t - 1:t, :]
        e_in = jnp.exp(acum)
        xdt_c = xdt[rows]
        xdt_b = xdt_c.astype(BF16)
        xout_b = (xdt_c * jnp.exp(tot - acum)).astype(BF16)
        acum_t = _dot3_left(at[:, rows], trit)
        y_parts = []
        for g in range(SSM_G):
            cg = cm[rows, g * SSM_N:(g + 1) * SSM_N].astype(BF16)
            bg = bm[rows, g * SSM_N:(g + 1) * SSM_N]
            gmat = _bdot_nt(cg, bg)
            st_g = st_ref[:, g * gw:(g + 1) * gw]
            y_g = _bdot(cg, st_g) * e_in[:, g * gw:(g + 1) * gw]
            xdt_g = xdt_b[:, g * gw:(g + 1) * gw]
            lhs, rhs = [], []
            for hh in range(hpg):
                h = g * hpg + hh
                col = acum[:, h * SSM_P:h * SSM_P + 1]
                row = acum_t[d * SSM_H + h:d * SSM_H + h + 1, :]
                lmat = jnp.exp(jnp.where(keep, col - row, -jnp.inf))
                lhs.append((gmat * lmat).astype(BF16))
                rhs.append(jnp.where((lane_g >> p_shift) == hh, xdt_g, jnp.zeros_like(xdt_g)))
            y_g = y_g + jnp.dot(jnp.concatenate(lhs, axis=1), jnp.concatenate(rhs, axis=0),
                                preferred_element_type=F32)
            y_parts.append(y_g)
            st_ref[:, g * gw:(g + 1) * gw] = (st_g * jnp.exp(tot[:, g * gw:(g + 1) * gw])
                                              + _bdot(bg.T, xout_b[:, g * gw:(g + 1) * gw]))
        y_rows[ci] = jnp.concatenate(y_parts, axis=-1)
    y_dir = jnp.concatenate(y_rows, axis=0)

    if not bwd:
        yacc_ref[pl.ds(row0, tb), :] = y_dir + xs * dexp_ref[...]
    else:
        y = (yacc_ref[pl.ds(row0, tb), :] + y_dir) * _silu(z_ref[...])
        parts = [_rms(y[:, g * gw:(g + 1) * gw], nrm_ref[:, g * gw:(g + 1) * gw]) for g in range(SSM_G)]
        y_ref[...] = jnp.concatenate(parts, axis=-1)


def _ssd_kernel(*refs, ns, cps, has_h0):
    n_in = 14
    if has_h0:
        h0f_ref, h0b_ref = refs[n_in:n_in + 2]
        refs = refs[:n_in] + refs[n_in + 2:]
    hf_ref, hb_ref = refs[n_in + 1:n_in + 3]
    block_refs = refs[:n_in + 1] + refs[n_in + 3:]
    st_ref = block_refs[n_in + 2]
    j = pl.program_id(1)

    @pl.when(j == 0)
    def _():
        st_ref[...] = h0f_ref[...] if has_h0 else jnp.zeros_like(st_ref)

    @pl.when(j == ns)
    def _():
        st_ref[...] = h0b_ref[...] if has_h0 else jnp.zeros_like(st_ref)

    @pl.when(j < ns)
    def _():
        _ssd_block(block_refs, j, bwd=False, ns=ns, cps=cps)

    @pl.when(j >= ns)
    def _():
        _ssd_block(block_refs, 2 * ns - 1 - j, bwd=True, ns=ns, cps=cps)

    @pl.when(j == ns - 1)
    def _():
        hf_ref[...] = st_ref[...]

    @pl.when(j == 2 * ns - 1)
    def _():
        hb_ref[...] = st_ref[...]


def _ssd(sz, sxbc, sdt, p, h0):
    b, n, _ = sz.shape
    t = SSM_CHUNK
    cps = min(SSD_CPS, n // t)
    tb = cps * t
    assert n % tb == 0
    ns = n // tb
    nb8 = n // SUBLANES
    tb8 = tb // SUBLANES
    has_h0 = h0 is not None
    sdt_t = jnp.swapaxes(sdt[:, :, :2 * SSM_H], 1, 2)
    sidx = lambda j: jnp.where(j >= ns, 2 * ns - 1 - j, j)
    fidx = lambda j: jnp.minimum(j, ns - 1)
    zidx = lambda j: jnp.where(j >= ns, 2 * ns - 1 - j, ns - 1)
    const2 = lambda bi, j: (0, 0)
    cw = jnp.zeros((SUBLANES, CONV_CH), F32).at[:CONV_W].set(p["ssm_conv_w"])
    dtb = jnp.zeros((1, DT_PAD), F32).at[0, :2 * SSM_H].set(p["ssm_dt_bias"].reshape(-1))
    dtbt = jnp.broadcast_to(p["ssm_dt_bias"].reshape(2 * SSM_H, 1), (2 * SSM_H, tb))
    alogt = jnp.broadcast_to(p["ssm_a_log"].reshape(2 * SSM_H, 1), (2 * SSM_H, tb))
    alogx = jnp.repeat(p["ssm_a_log"], SSM_P, axis=-1)
    dexp = jnp.repeat(p["ssm_d"], SSM_P).reshape(1, SSM_DI)
    in_specs = [pl.BlockSpec((None, tb, CONV_CH), lambda bi, j: (bi, fidx(j), 0)),
                pl.BlockSpec((None, SUBLANES, CONV_CH), lambda bi, j: (bi, jnp.maximum(fidx(j) * tb8 - 1, 0), 0)),
                pl.BlockSpec((None, SUBLANES, CONV_CH),
                             lambda bi, j: (bi, jnp.minimum((fidx(j) + 1) * tb8, nb8 - 1), 0)),
                pl.BlockSpec((None, tb, SSM_DI), lambda bi, j: (bi, zidx(j), 0)),
                pl.BlockSpec((None, tb, DT_PAD), lambda bi, j: (bi, sidx(j), 0)),
                pl.BlockSpec((None, 2 * SSM_H, tb), lambda bi, j: (bi, 0, sidx(j))),
                pl.BlockSpec((SUBLANES, CONV_CH), const2),
                pl.BlockSpec((1, CONV_CH), const2),
                pl.BlockSpec((1, DT_PAD), const2),
                pl.BlockSpec((2 * SSM_H, tb), const2),
                pl.BlockSpec((2 * SSM_H, tb), const2),
                pl.BlockSpec((2, SSM_DI), const2),
                pl.BlockSpec((1, SSM_DI), const2),
                pl.BlockSpec((1, SSM_DI), const2)]
    args = [sxbc, sxbc, sxbc, sz, sdt, sdt_t, cw, p["ssm_conv_b"].reshape(1, CONV_CH), dtb, dtbt,
            alogt, alogx, dexp, p["ssm_norm"].reshape(1, SSM_DI)]
    st_spec = pl.BlockSpec((None, SSM_N, SSM_DI), lambda bi, j: (bi, 0, 0))
    if has_h0:
        in_specs += [st_spec, st_spec]
        args += list(h0)
    y, hf, hb = pl.pallas_call(
        functools.partial(_ssd_kernel, ns=ns, cps=cps, has_h0=has_h0),
        grid=(b, 2 * ns),
        in_specs=in_specs,
        out_specs=[pl.BlockSpec((None, tb, SSM_DI), lambda bi, j: (bi, zidx(j), 0)), st_spec, st_spec],
        out_shape=[jax.ShapeDtypeStruct((b, n, SSM_DI), F32),
                   jax.ShapeDtypeStruct((b, SSM_N, SSM_DI), F32),
                   jax.ShapeDtypeStruct((b, SSM_N, SSM_DI), F32)],
        scratch_shapes=[pltpu.VMEM((tb + 2 * SUBLANES, CONV_CH), F32),
                        pltpu.VMEM((SSM_N, SSM_DI), F32),
                        pltpu.VMEM((n, SSM_DI), F32),
                        pltpu.VMEM((n, CONV_CH), F32)],
        compiler_params=_cparams(("arbitrary", "arbitrary")),
        name="ssd_lat" if has_h0 else "ssd_ctx",
    )(*args)
    return y, hf, hb


def _post_kernel(x_ref, oda_ref, ossm_ref, ona_ref, gate_ref, mod_ref, wda_ref, wssm_ref, wna_ref, wout_ref,
                 gpost_ref, gpre_ref, wr_ref, br_ref, x1_ref, h2_ref, rg_ref):
    d = x_ref.shape[1]
    tm = x_ref.shape[0]
    mix = (gate_ref[:, 0:d] * _bdot(oda_ref[...], wda_ref[...])
           + gate_ref[:, d:2 * d] * _bdot(ossm_ref[...], wssm_ref[...])
           + gate_ref[:, 2 * d:3 * d] * _bdot(ona_ref[...], wna_ref[...]))
    mix = _bdot(mix, wout_ref[...])
    x1 = x_ref[...] + mod_ref[2:3, :] * _rms(mix, gpost_ref[...])
    x1_ref[...] = x1
    h2 = _rms(x1, gpre_ref[...]) * (1.0 + mod_ref[4:5, :]) + mod_ref[3:4, :]
    h_hi = h2.astype(BF16)
    h2_ref[...] = h_hi

    h_mid = (h2 - h_hi.astype(F32)).astype(BF16)
    a = jnp.dot(h_hi, wr_ref[...], preferred_element_type=F32)
    b = jnp.dot(h_mid, wr_ref[:, :LANES], preferred_element_type=F32)
    s = _sigmoid(a[:, :LANES] + (a[:, LANES:] + b))
    work = s + br_ref[...]
    col = lax.broadcasted_iota(jnp.int32, (tm, LANES), 1).astype(F32)
    picked = jnp.zeros((tm, LANES), F32)
    for _ in range(TOP_K):
        m = jnp.max(work, axis=-1, keepdims=True)
        first = jnp.min(jnp.where(work == m, col, float(LANES)), axis=-1, keepdims=True)
        hit = col == first
        picked = jnp.where(hit, 1.0, picked)
        work = jnp.where(hit, -jnp.inf, work)
    sel = picked * s
    g = sel / jnp.sum(sel, axis=-1, keepdims=True) * ROUTE_SCALE
    g_hi = g.astype(BF16).astype(F32)
    r1 = g - g_hi
    g_mid = r1.astype(BF16).astype(F32)
    g_lo = r1 - g_mid
    packed = g_hi + pltpu.roll(g_mid, N_EXPERTS, 1) + pltpu.roll(g_lo, 2 * N_EXPERTS, 1)
    rg_ref[...] = packed.astype(rg_ref.dtype)


def _post(x3, o_da, o_ssm, o_na, gates, mods, p):
    b, n, d = x3.shape
    tm = min(POST_TM, n)
    tps = n // tm
    t = b * n
    per_batch_mods = mods.shape[0] > 1
    mod_idx = (lambda i: (i // tps, 0, 0)) if per_batch_mods else (lambda i: (0, 0, 0))
    row = lambda i: (i, 0)
    const2 = lambda i: (0, 0)
    flat = lambda a: a.reshape(t, a.shape[-1])
    wspec = lambda k: pl.BlockSpec((k, d), const2)
    w_r = jnp.zeros((d, LANES), F32).at[:, :N_EXPERTS].set(p["w_router"])
    w_r_hi = w_r.astype(BF16)
    w_router = jnp.concatenate([w_r_hi, (w_r - w_r_hi.astype(F32)).astype(BF16)], axis=1)
    b_router = jnp.full((1, LANES), -jnp.inf, F32).at[0, :N_EXPERTS].set(p["b_router"])
    x1, h2, route = pl.pallas_call(
        _post_kernel,
        grid=(t // tm,),
        in_specs=[pl.BlockSpec((tm, d), row), pl.BlockSpec((tm, DA_W), row), pl.BlockSpec((tm, SSM_DI), row),
                  pl.BlockSpec((tm, NA_W), row), pl.BlockSpec((tm, 3 * d), row),
                  pl.BlockSpec((None, 6, d), mod_idx),
                  wspec(DA_W), wspec(SSM_DI), wspec(NA_W), wspec(d),
                  pl.BlockSpec((1, d), const2), pl.BlockSpec((1, d), const2),
                  pl.BlockSpec((d, 2 * LANES), const2), pl.BlockSpec((1, LANES), const2)],
        out_specs=[pl.BlockSpec((tm, d), row), pl.BlockSpec((tm, d), row), pl.BlockSpec((tm, LANES), row)],
        out_shape=[jax.ShapeDtypeStruct((t, d), F32), jax.ShapeDtypeStruct((t, d), BF16),
                   jax.ShapeDtypeStruct((t, LANES), BF16)],
        compiler_params=_cparams(("arbitrary",)),
        name="post",
    )(flat(x3), flat(o_da), flat(o_ssm), flat(o_na), flat(gates), mods,
      p["w_br_da"], p["w_br_ssm"], p["w_br_na"], p["w_out"],
      p["g_post1"].reshape(1, d), p["g_pre2"].reshape(1, d), w_router, b_router)
    return x1, h2, route


def _moe_kernel(x1_ref, h2_ref, rg_ref, mod_ref, wsg_ref, wsu_ref, wsd_ref, weg_ref, weu_ref, wed_ref,
                gpost_ref, o_ref, acc_ref):
    e = pl.program_id(1)
    hb = h2_ref[...]

    @pl.when(e == 0)
    def _():
        acc_ref[...] = _bdot(_silu(_bdot(hb, wsg_ref[...])) * _bdot(hb, wsu_ref[...]), wsd_ref[...])

    width = MOE_EG * EXPERT_FF
    r_i = lax.broadcasted_iota(jnp.int32, (LANES, width), 0)
    c_i = lax.broadcasted_iota(jnp.int32, (LANES, width), 1)
    ff_shift = EXPERT_FF.bit_length() - 1
    onehot = jnp.where((r_i < 3 * N_EXPERTS) & ((r_i & (N_EXPERTS - 1)) == e * MOE_EG + (c_i >> ff_shift)),
                       1.0, 0.0).astype(BF16)
    g_exp = jnp.dot(rg_ref[...], onehot, preferred_element_type=F32)
    act = _silu(_bdot(hb, weg_ref[...])) * _bdot(hb, weu_ref[...]) * g_exp
    acc_ref[...] += _bdot(act, wed_ref[...])

    @pl.when(e == pl.num_programs(1) - 1)
    def _():
        o_ref[...] = x1_ref[...] + mod_ref[5:6, :] * _rms(acc_ref[...], gpost_ref[...])


def _moe(x1, h2, route, mods, p, n):
    t, d = x1.shape
    tm = MOE_TM
    per_batch_mods = mods.shape[0] > 1
    if per_batch_mods:
        assert n % tm == 0
    tps = max(n // tm, 1)
    mod_idx = (lambda i, e: (i // tps, 0, 0)) if per_batch_mods else (lambda i, e: (0, 0, 0))
    row = lambda i, e: (i, 0)
    const2 = lambda i, e: (0, 0)
    width = MOE_EG * EXPERT_FF
    resident = dict(pipeline_mode=pl.Buffered(1)) if MOE_EG == N_EXPERTS else {}
    return pl.pallas_call(
        _moe_kernel,
        grid=(t // tm, N_EXPERTS // MOE_EG),
        in_specs=[pl.BlockSpec((tm, d), row), pl.BlockSpec((tm, d), row), pl.BlockSpec((tm, LANES), row),
                  pl.BlockSpec((None, 6, d), mod_idx),
                  pl.BlockSpec((d, SHARED_FF), const2), pl.BlockSpec((d, SHARED_FF), const2),
                  pl.BlockSpec((SHARED_FF, d), const2),
                  pl.BlockSpec((d, width), lambda i, e: (0, e), **resident),
                  pl.BlockSpec((d, width), lambda i, e: (0, e), **resident),
                  pl.BlockSpec((width, d), lambda i, e: (e, 0), **resident),
                  pl.BlockSpec((1, d), const2)],
        out_specs=pl.BlockSpec((tm, d), row),
        out_shape=jax.ShapeDtypeStruct((t, d), F32),
        scratch_shapes=[pltpu.VMEM((tm, d), F32)],
        compiler_params=_cparams(("arbitrary", "arbitrary")),
        name="moe",
    )(x1, h2, route, mods,
      p["w_s_gate"], p["w_s_up"], p["w_s_down"], p["w_e_gate"], p["w_e_up"], p["w_e_down"],
      p["g_post2"].reshape(1, d))


def _rope_tables(n):
    pos = jnp.arange(n)
    rows = (pos // GRID_W).astype(F32)
    cols = (pos % GRID_W).astype(F32)
    n_freq = DA_HD // 4
    inv = ROPE_BASE ** (-jnp.arange(n_freq, dtype=F32) / n_freq)
    ar, ac = rows[:, None] * inv, cols[:, None] * inv
    zero = jnp.zeros_like(ar)

    def lanes(first_r, second_r, first_c, second_c):
        half = jnp.concatenate([first_r, second_r, first_c, second_c], axis=-1)
        return jnp.concatenate([half, half], axis=-1)

    cos = lanes(jnp.cos(ar), jnp.cos(ar), jnp.cos(ac), jnp.cos(ac))
    s_first = lanes(-jnp.sin(ar), zero, -jnp.sin(ac), zero)
    s_second = lanes(zero, jnp.sin(ar), zero, jnp.sin(ac))
    return cos, s_first, s_second


def _prep_layer_params(stacked, layer):
    p = {name: arr[layer] for name, arr in stacked.items()}
    w_in = p["w_in"]
    dt0 = 3 * DA_W + SSM_DI + CONV_CH
    d = w_in.shape[0]
    p["w_in_p"] = jnp.concatenate(
        [w_in[:, :dt0], w_in[:, dt0 + 2 * SSM_H:], w_in[:, dt0:dt0 + 2 * SSM_H],
         jnp.zeros((d, DT_PAD - 2 * SSM_H), w_in.dtype)], axis=-1).astype(BF16)
    for name in ("w_gate", "w_br_da", "w_br_ssm", "w_br_na", "w_out", "w_s_gate", "w_s_up", "w_s_down"):
        p[name] = p[name].astype(BF16)
    ff = N_EXPERTS * EXPERT_FF
    p["w_e_gate"] = p["w_e_gate"].astype(BF16).transpose(1, 0, 2).reshape(d, ff)
    p["w_e_up"] = p["w_e_up"].astype(BF16).transpose(1, 0, 2).reshape(d, ff)
    p["w_e_down"] = p["w_e_down"].astype(BF16).reshape(ff, d)
    return p


def _trunk_layer(x3, mods, p, layer, ctx):
    b, n, d = x3.shape
    latent = ctx is not None
    rope_tabs = ctx[6] if latent else None
    qkv_dtype = BF16 if latent else F32
    outs = _pre(x3, mods, p["g_pre1"], p["w_in_p"], p["w_gate"], rope_tabs, qkv_dtype)
    dq, dk, dv, sz, sxbc, nq, nk, nv, sdt, gates = outs[:10]
    if latent:
        ck, cv, cnk, cnv, h0f, h0b, _, na_bias = ctx
        o_da = _da(dq, dk, dv, p["da_lambda"], p["da_subln"], layer, (ck, cv, outs[10]))
        o_na = _na_lat(nq, nk, nv, cnk, cnv, na_bias, layer)
        o_ssm, hf, hb = _ssd(sz, sxbc, sdt, p, (h0f, h0b))
    else:
        o_da = _da(dq, dk, dv, p["da_lambda"], p["da_subln"], layer, None)
        o_na = _na_ctx(nq, nk, nv)
        o_ssm, hf, hb = _ssd(sz, sxbc, sdt, p, None)
    x1, h2, route = _post(x3, o_da, o_ssm, o_na, gates, mods, p)
    x2 = _moe(x1, h2, route, mods, p, n).reshape(b, n, d)
    state = None
    if not latent:
        to_state = lambda s: s.reshape(b, SSM_N, SSM_H, SSM_P).transpose(0, 2, 3, 1)
        state = (dk.reshape(b, n, DA_H, 2, DA_HD).transpose(0, 2, 1, 3, 4),
                 dv.reshape(b, n, DA_H, 2 * DA_HD).transpose(0, 2, 1, 3),
                 nk.reshape(b, n, NA_H, NA_HD).transpose(0, 2, 1, 3),
                 nv.reshape(b, n, NA_H, NA_HD).transpose(0, 2, 1, 3),
                 to_state(hf), to_state(hb))
    return x2, state


def kernel(x_prompt, x_sample, c, cache_da_k, cache_da_v, cache_na_k, cache_na_v, state_ssm_f, state_ssm_b,
           c_ctx, w_ada, b_ada, g_pre1, g_post1, g_pre2, g_post2, w_in, w_gate, da_lambda, da_subln,
           ssm_conv_w, ssm_conv_b, ssm_dt_bias, ssm_a_log, ssm_d, ssm_norm, na_rpb, w_br_da, w_br_ssm,
           w_br_na, w_out, w_router, b_router, w_e_gate, w_e_up, w_e_down, w_s_gate, w_s_up, w_s_down):
    stacked = {
        "g_pre1": g_pre1, "g_post1": g_post1, "g_pre2": g_pre2, "g_post2": g_post2, "w_in": w_in,
        "w_gate": w_gate, "da_lambda": da_lambda, "da_subln": da_subln, "ssm_conv_w": ssm_conv_w,
        "ssm_conv_b": ssm_conv_b, "ssm_dt_bias": ssm_dt_bias, "ssm_a_log": ssm_a_log, "ssm_d": ssm_d,
        "ssm_norm": ssm_norm, "na_rpb": na_rpb, "w_br_da": w_br_da, "w_br_ssm": w_br_ssm, "w_br_na": w_br_na,
        "w_out": w_out, "w_router": w_router, "b_router": b_router, "w_e_gate": w_e_gate, "w_e_up": w_e_up,
        "w_e_down": w_e_down, "w_s_gate": w_s_gate, "w_s_up": w_s_up, "w_s_down": w_s_down,
    }
    depth = w_ada.shape[0]
    nb = x_sample.shape[0]
    d = x_prompt.shape[-1]
    rows = -(-(1 + nb) // SUBLANES) * SUBLANES
    cvec = jnp.zeros((rows, d), F32).at[0].set(c_ctx).at[1:1 + nb].set(c)
    mods = _ada(cvec, w_ada, b_ada).reshape(depth, rows, 6, d)

    past = cache_da_k.shape[3]
    ck_all = cache_da_k.reshape(nb, depth, DA_H, past, 2 * DA_HD)
    pair = lambda a: a.reshape(nb, depth, NA_H // 2, 2, past, NA_HD).transpose(0, 1, 2, 4, 3, 5).reshape(
        nb, depth, NA_H // 2, past, 2 * NA_HD)
    cnk_all, cnv_all = pair(cache_na_k), pair(cache_na_v)
    st_t = lambda s: s.transpose(0, 1, 4, 2, 3).reshape(nb, depth, SSM_N, SSM_DI)
    h0f_all, h0b_all = st_t(state_ssm_f), st_t(state_ssm_b)

    n_lat = x_sample.shape[1]
    rope_tabs = _rope_tables(n_lat)
    na_bias = _na_bias_tables(na_rpb.reshape((depth * NA_H,) + na_rpb.shape[2:]), n_lat // GRID_W)
    na_bias = na_bias.reshape((depth, NA_H) + na_bias.shape[1:])

    xp, xs = x_prompt, x_sample
    new = ([], [], [], [], [], [])
    for layer in range(depth):
        p = _prep_layer_params(stacked, layer)
        xp, st = _trunk_layer(xp, mods[layer, 0:1], p, layer, None)
        for lst, arr in zip(new, st):
            lst.append(arr)
        ctx = (ck_all, cache_da_v, cnk_all, cnv_all, h0f_all[:, layer], h0b_all[:, layer], rope_tabs, na_bias)
        xs, _ = _trunk_layer(xs, mods[layer, 1:1 + nb], p, layer, ctx)
    return (xp, xs) + tuple(jnp.stack(lst, axis=1) for lst in new)
```
